```python
import jax, jax.numpy as jnp
from jax import lax
import numpy as np

D_MODEL = 1024
BATCH = 2
SEQ = 8192
DEPTH = 2

N_A_LAYERS = DEPTH // 2
N_B_LAYERS = DEPTH - N_A_LAYERS
N_DENSE = (DEPTH + 1) // 2
N_MOE = DEPTH // 2

DN_HEADS = 8
DN_DK = 128
DN_DV = 128
DN_CONV = 4
DN_CHUNK = 64

SW_Q_HEADS = 16
SW_KV_HEADS = 4
SW_HEAD_DIM = 64
SW_WINDOW = 128
SW_BLOCK = 128

D_FF = 2816
N_EXPERTS = 8
TOP_K = 2
D_EXPERT = 2816

DEEP_ALPHA = (2 * DEPTH) ** 0.25
DEEP_BETA = (8 * DEPTH) ** -0.25
LN_EPS = 1e-5
RMS_EPS = 1e-6

kernel_name = "yoco_deltanet_swa_sink_alibi_moe_deepnorm"


def layer_norm(x, g, b):
    xf = x.astype(jnp.float32)
    mu = jnp.mean(xf, axis=-1, keepdims=True)
    xc = xf - mu
    var = jnp.mean(xc * xc, axis=-1, keepdims=True)
    y = xc * lax.rsqrt(var + LN_EPS) * g.astype(jnp.float32) + b.astype(jnp.float32)
    return y.astype(x.dtype)


def l2_normalize(a):
    return a * lax.rsqrt(jnp.sum(a * a, axis=-1, keepdims=True) + RMS_EPS)


def causal_short_conv(x, w):
    K = w.shape[0]
    T = x.shape[1]
    xp = jnp.pad(x, ((0, 0), (K - 1, 0), (0, 0)))
    return sum(xp[:, i:i + T] * w[i] for i in range(K))


def gated_delta_chunked(q, k, v, g, beta):
    B_, T, H, dk = q.shape
    dv = v.shape[-1]
    C = DN_CHUNK
    N = T // C

    def chunks(a):
        a = a.reshape((B_, N, C, H) + a.shape[3:])
        return jnp.moveaxis(a, 3, 1)

    q, k, v, g, beta = chunks(q), chunks(k), chunks(v), chunks(g), chunks(beta)
    G = jnp.cumsum(g, axis=-1)
    idx = jnp.arange(C)
    incl = idx[:, None] >= idx[None, :]
    strict = idx[:, None] > idx[None, :]
    decay = jnp.exp(jnp.where(incl, G[..., :, None] - G[..., None, :], -jnp.inf))

    kb = k * beta[..., None]
    vb = v * beta[..., None]
    L = jnp.where(strict, jnp.einsum('bhncd,bhnsd->bhncs', kb, k) * decay, 0.0)
    eye = jnp.eye(C, dtype=q.dtype)
    rhs = jnp.concatenate([vb, kb * jnp.exp(G)[..., None]], axis=-1)
    sol = lax.linalg.triangular_solve(eye + L, rhs, left_side=True, lower=True,
                                      unit_diagonal=True)
    U = sol[..., :dv]
    W = sol[..., dv:]

    Aqk = jnp.einsum('bhncd,bhnsd->bhncs', q, k) * decay
    qg = q * jnp.exp(G)[..., None]
    kdec = k * jnp.exp(G[..., -1:] - G)[..., None]
    glast = jnp.exp(G[..., -1])

    xs = tuple(jnp.moveaxis(a, 2, 0) for a in (U, W, qg, Aqk, kdec, glast))

    def step(S, inp):
        u, w, qc, aqk, kd, gl = inp
        v_new = u - jnp.einsum('bhck,bhkv->bhcv', w, S)
        o = jnp.einsum('bhck,bhkv->bhcv', qc, S) + jnp.einsum('bhcs,bhsv->bhcv', aqk, v_new)
        S = S * gl[..., None, None] + jnp.einsum('bhck,bhcv->bhkv', kd, v_new)
        return S, o

    S0 = jnp.zeros((B_, H, dk, dv), q.dtype)
    _, o = lax.scan(step, S0, xs)
    return o.transpose(1, 0, 3, 2, 4).reshape(B_, T, H, dv)


def deltanet_mixer(x, w_in, conv_w, a_log, dt_bias, norm_w, w_out):
    B_, T, _ = x.shape
    H, dk, dv = DN_HEADS, DN_DK, DN_DV
    proj = x @ w_in
    n_qkv = 2 * H * dk + H * dv
    qkv = proj[..., :n_qkv]
    z = proj[..., n_qkv:n_qkv + H * dv]
    b_logit = proj[..., n_qkv + H * dv:n_qkv + H * dv + H]
    a_logit = proj[..., n_qkv + H * dv + H:]
    qkv = jax.nn.silu(causal_short_conv(qkv, conv_w)).astype(jnp.float32)
    q = qkv[..., :H * dk].reshape(B_, T, H, dk)
    k = qkv[..., H * dk:2 * H * dk].reshape(B_, T, H, dk)
    v = qkv[..., 2 * H * dk:].reshape(B_, T, H, dv)
    q = l2_normalize(q) * (dk ** -0.5)
    k = l2_normalize(k)
    beta = jax.nn.sigmoid(b_logit.astype(jnp.float32))
    g = -jnp.exp(a_log.astype(jnp.float32)) * jax.nn.softplus(
        a_logit.astype(jnp.float32) + dt_bias.astype(jnp.float32))
    o = gated_delta_chunked(q, k, v, g, beta)
    o = o * lax.rsqrt(jnp.mean(o * o, axis=-1, keepdims=True) + RMS_EPS) * norm_w.astype(jnp.float32)
    o = o * jax.nn.silu(z.reshape(B_, T, H, dv).astype(jnp.float32))
    return o.reshape(B_, T, H * dv).astype(x.dtype) @ w_out


def shared_kv(x, w_kv):
    B_, T, _ = x.shape
    kv = x @ w_kv
    n = SW_KV_HEADS * SW_HEAD_DIM
    k = kv[..., :n].reshape(B_, T, SW_KV_HEADS, SW_HEAD_DIM)
    v = kv[..., n:].reshape(B_, T, SW_KV_HEADS, SW_HEAD_DIM)
    return k, v


def alibi_slopes(n):
    return 2.0 ** (-8.0 * jnp.arange(1, n + 1, dtype=jnp.float32) / n)


def sliding_window_mixer(x, k_sh, v_sh, w_q, sinks, w_out):
    B_, T, _ = x.shape
    Hq, Hk, hd = SW_Q_HEADS, SW_KV_HEADS, SW_HEAD_DIM
    G = Hq // Hk
    W = SW_BLOCK
    nb = T // W
    q = (x @ w_q).astype(jnp.float32).reshape(B_, nb, W, Hk, G, hd) * (hd ** -0.5)

    def band(a):
        a = a.astype(jnp.float32)
        ap = jnp.pad(a, ((0, 0), (W, 0), (0, 0), (0, 0)))
        prev = ap[:, :T].reshape(B_, nb, W, Hk, hd)
        cur = a.reshape(B_, nb, W, Hk, hd)
        return jnp.concatenate([prev, cur], axis=2)

    kb, vb = band(k_sh), band(v_sh)
    s = jnp.einsum('bnqkgd,bnskd->bnkgqs', q, kb)
    qi = jnp.arange(W)
    sj = jnp.arange(2 * W)
    dist = qi[:, None] + W - sj[None, :]
    kpos = jnp.arange(nb)[:, None] * W - W + sj[None, :]
    valid = ((dist >= 0) & (dist < SW_WINDOW))[None] & (kpos >= 0)[:, None, :]
    slopes = alibi_slopes(Hq).reshape(Hk, G)
    s = s - slopes[:, :, None, None] * dist.astype(jnp.float32)
    s = jnp.where(valid[None, :, None, None], s, -jnp.inf)
    sink = sinks.astype(jnp.float32).reshape(Hk, G)[:, :, None, None]
    m = jnp.maximum(jnp.max(s, axis=-1, keepdims=True), sink)
    p = jnp.exp(s - m)
    p = p / (jnp.sum(p, axis=-1, keepdims=True) + jnp.exp(sink - m))
    o = jnp.einsum('bnkgqs,bnskd->bnqkgd', p, vb).reshape(B_, T, Hq * hd)
    return o.astype(x.dtype) @ w_out


def dense_swiglu(x, w_gu, w_down):
    h = x @ w_gu
    gate, up = h[..., :D_FF], h[..., D_FF:]
    return (jax.nn.silu(gate) * up) @ w_down


def moe_swiglu(x, w_router, w_gu, w_down):
    B_, T, D = x.shape
    xt = x.reshape(-1, D)
    logits = (xt @ w_router).astype(jnp.float32)
    top_v, top_i = lax.top_k(logits, TOP_K)
    top_w = jax.nn.softmax(top_v, axis=-1)
    gates = jnp.sum(jax.nn.one_hot(top_i, N_EXPERTS, dtype=jnp.float32) * top_w[..., None], axis=1)
    out = jnp.zeros_like(xt)
    for e in range(N_EXPERTS):
        h = xt @ w_gu[e]
        y = (jax.nn.silu(h[:, :D_EXPERT]) * h[:, D_EXPERT:]) @ w_down[e]
        out = out + gates[:, e:e + 1].astype(xt.dtype) * y
    return out.reshape(B_, T, D)


def setup_inputs(seed: int = 0) -> dict:
    key = jax.random.key(seed)
    ks = jax.random.split(key, 20)
    f32 = jnp.float32
    D = D_MODEL

    def nrm(k, shape, scale):
        return jax.random.normal(k, shape, f32) * scale

    dn_in_width = 2 * DN_HEADS * DN_DK + 2 * DN_HEADS * DN_DV + 2 * DN_HEADS
    dt = jnp.exp(jax.random.uniform(ks[3], (N_A_LAYERS, DN_HEADS), f32,
                                    math_log(1e-3), math_log(1e-1)))
    dt_bias = dt + jnp.log(-jnp.expm1(-dt))
    n_kv = SW_KV_HEADS * SW_HEAD_DIM
    w_k = nrm(ks[7], (D, n_kv), D ** -0.5)
    w_v = nrm(ks[8], (D, n_kv), D ** -0.5 * DEEP_BETA)
    return {
        "x": jax.random.normal(ks[0], (BATCH, SEQ, D), f32),
        "dn_w_in": nrm(ks[1], (N_A_LAYERS, D, dn_in_width), D ** -0.5),
        "dn_conv_w": nrm(ks[2], (N_A_LAYERS, DN_CONV, 2 * DN_HEADS * DN_DK + DN_HEADS * DN_DV), DN_CONV ** -0.5),
        "dn_a_log": jnp.log(jax.random.uniform(ks[4], (N_A_LAYERS, DN_HEADS), f32, 1.0, 16.0)),
        "dn_dt_bias": dt_bias,
        "dn_norm_w": 1.0 + nrm(ks[5], (N_A_LAYERS, DN_DV), 0.02),
        "dn_w_out": nrm(ks[6], (N_A_LAYERS, DN_HEADS * DN_DV, D), (DN_HEADS * DN_DV) ** -0.5 * DEEP_BETA),
        "w_kv_shared": jnp.concatenate([w_k, w_v], axis=1),
        "sw_w_q": nrm(ks[9], (N_B_LAYERS, D, SW_Q_HEADS * SW_HEAD_DIM), D ** -0.5),
        "sw_sinks": nrm(ks[10], (N_B_LAYERS, SW_Q_HEADS), 1.0),
        "sw_w_out": nrm(ks[11], (N_B_LAYERS, SW_Q_HEADS * SW_HEAD_DIM, D), (SW_Q_HEADS * SW_HEAD_DIM) ** -0.5 * DEEP_BETA),
        "ffn_w_gu": nrm(ks[12], (N_DENSE, D, 2 * D_FF), D ** -0.5),
        "ffn_w_down": nrm(ks[13], (N_DENSE, D_FF, D), D_FF ** -0.5 * DEEP_BETA),
        "moe_w_router": nrm(ks[14], (N_MOE, D, N_EXPERTS), D ** -0.5),
        "moe_w_gu": nrm(ks[15], (N_MOE, N_EXPERTS, D, 2 * D_EXPERT), D ** -0.5),
        "moe_w_down": nrm(ks[16], (N_MOE, N_EXPERTS, D_EXPERT, D), D_EXPERT ** -0.5 * DEEP_BETA),
        "ln_g": 1.0 + nrm(ks[17], (DEPTH, 2, D), 0.02),
        "ln_b": nrm(ks[18], (DEPTH, 2, D), 0.02),
    }


def math_log(v):
    return float(np.log(v))


def reference(x, dn_w_in, dn_conv_w, dn_a_log, dn_dt_bias, dn_norm_w, dn_w_out,
              w_kv_shared, sw_w_q, sw_sinks, sw_w_out, ffn_w_gu, ffn_w_down,
              moe_w_router, moe_w_gu, moe_w_down, ln_g, ln_b):
    k_sh, v_sh = None, None
    for layer in range(DEPTH):
        if layer < N_A_LAYERS:
            i = layer
            h = deltanet_mixer(x, dn_w_in[i], dn_conv_w[i], dn_a_log[i], dn_dt_bias[i],
                               dn_norm_w[i], dn_w_out[i])
        else:
            if layer == N_A_LAYERS:
                k_sh, v_sh = shared_kv(x, w_kv_shared)
            i = layer - N_A_LAYERS
            h = sliding_window_mixer(x, k_sh, v_sh, sw_w_q[i], sw_sinks[i], sw_w_out[i])
        x = layer_norm(DEEP_ALPHA * x + h, ln_g[layer, 0], ln_b[layer, 0])
        j = layer // 2
        if layer % 2 == 0:
            h = dense_swiglu(x, ffn_w_gu[j], ffn_w_down[j])
        else:
            h = moe_swiglu(x, moe_w_router[j], moe_w_gu[j], moe_w_down[j])
        x = layer_norm(DEEP_ALPHA * x + h, ln_g[layer, 1], ln_b[layer, 1])
    return x
```

```python
import functools

import jax
import jax.numpy as jnp
from jax import lax
from jax.experimental import pallas as pl
from jax.experimental.pallas import tpu as pltpu

F32 = jnp.float32
BF16 = jnp.bfloat16
I32 = jnp.int32

DEPTH = 2
DN_HEADS = 8
DN_DK = 128
DN_DV = 128
DN_CONV = 4
DN_CHUNK = 64
SW_Q_HEADS = 16
SW_KV_HEADS = 4
SW_HEAD_DIM = 64
SW_BLOCK = 128
N_EXPERTS = 8
DEEP_ALPHA = (2 * DEPTH) ** 0.25
LN_EPS = 1e-5
RMS_EPS = 1e-6

V7X_VMEM_BYTES = 64 * 1024 * 1024
V7X_LANES = 128
V7X_BF16_SUBLANES = 16

VMEM_LIMIT = V7X_VMEM_BYTES - 8 * 1024 * 1024
ROW_TILE = 512
DN_ROWS = 256
MOE_TILE = 512
MOE_CHUNK = 256
MOE_ALIGN = V7X_BF16_SUBLANES
COPY_SIZES = (256, 128, 64, 32, 16)
FF_CHUNKS = ((0, 1536), (1536, 1280))


def _cparams(*sem):
    return pltpu.CompilerParams(dimension_semantics=sem, vmem_limit_bytes=VMEM_LIMIT)


def _dot(a, b):
    return jnp.dot(a, b, preferred_element_type=F32)


def _dot_nt(a, b):
    return lax.dot_general(a, b, (((1,), (1,)), ((), ())), preferred_element_type=F32)


def _dot_tn(a, b):
    return lax.dot_general(a, b, (((0,), (0,)), ((), ())), preferred_element_type=F32)


def _silu(x):
    return x * (1.0 / (1.0 + jnp.exp(-x)))


def _softplus(x):
    return jnp.maximum(x, 0.0) + jnp.log1p(jnp.exp(-jnp.abs(x)))


def _layer_norm(y, g, b):
    mu = jnp.mean(y, axis=-1, keepdims=True)
    yc = y - mu
    var = jnp.mean(yc * yc, axis=-1, keepdims=True)
    return yc * lax.rsqrt(var + LN_EPS) * g + b


def _dn_inproj_body(x_ref, w_ref, wba_ref, wbat_ref, proj_ref, ba_ref, bat_ref):
    xb = x_ref[...].astype(BF16)
    n_out = w_ref.shape[1]
    for c0 in range(0, n_out, 1024):
        proj_ref[:, c0:c0 + 1024] = _dot(xb, w_ref[:, c0:c0 + 1024])
    ba_ref[...] = _dot(xb, wba_ref[...])
    bat_ref[...] = _dot_nt(wbat_ref[...], xb)


def _dn_inproj(x, w_main, w_ba, w_bat):
    n, d = x.shape
    n_out = w_main.shape[1]
    tm = ROW_TILE
    return pl.pallas_call(
        _dn_inproj_body,
        grid=(n // tm,),
        in_specs=[
            pl.BlockSpec((tm, d), lambda i: (i, 0)),
            pl.BlockSpec((d, n_out), lambda i: (0, 0)),
            pl.BlockSpec((d, V7X_LANES), lambda i: (0, 0)),
            pl.BlockSpec((2 * DN_HEADS, d), lambda i: (0, 0)),
        ],
        out_specs=[
            pl.BlockSpec((tm, n_out), lambda i: (i, 0)),
            pl.BlockSpec((tm, V7X_LANES), lambda i: (i, 0)),
            pl.BlockSpec((2 * DN_HEADS, tm), lambda i: (0, i)),
        ],
        out_shape=[
            jax.ShapeDtypeStruct((n, n_out), F32),
            jax.ShapeDtypeStruct((n, V7X_LANES), F32),
            jax.ShapeDtypeStruct((2 * DN_HEADS, n), F32),
        ],
        compiler_params=_cparams("parallel"),
        name="dn_inproj",
    )(x, w_main, w_ba, w_bat)


def _proj_body(x_ref, w_ref, o_ref):
    o_ref[...] = _dot(x_ref[...].astype(BF16), w_ref[...])


def _proj(x, w):
    n, k = x.shape
    m = w.shape[1]
    tm = ROW_TILE
    return pl.pallas_call(
        _proj_body,
        grid=(n // tm,),
        in_specs=[pl.BlockSpec((tm, k), lambda i: (i, 0)),
                  pl.BlockSpec((k, m), lambda i: (0, 0))],
        out_specs=pl.BlockSpec((tm, m), lambda i: (i, 0)),
        out_shape=jax.ShapeDtypeStruct((n, m), F32),
        compiler_params=_cparams("parallel"),
        name="proj",
    )(x, w)


def _proj_ln_body(a_ref, w_ref, res_ref, g_ref, b_ref, o_ref):
    h = _dot(a_ref[...].astype(BF16), w_ref[...])
    o_ref[...] = _layer_norm(DEEP_ALPHA * res_ref[...] + h, g_ref[...], b_ref[...])


def _proj_ln(a, w, res, g, b):
    n, k = a.shape
    d = w.shape[1]
    tm = ROW_TILE
    return pl.pallas_call(
        _proj_ln_body,
        grid=(n // tm,),
        in_specs=[pl.BlockSpec((tm, k), lambda i: (i, 0)),
                  pl.BlockSpec((k, d), lambda i: (0, 0)),
                  pl.BlockSpec((tm, d), lambda i: (i, 0)),
                  pl.BlockSpec((1, d), lambda i: (0, 0)),
                  pl.BlockSpec((1, d), lambda i: (0, 0))],
        out_specs=pl.BlockSpec((tm, d), lambda i: (i, 0)),
        out_shape=jax.ShapeDtypeStruct((n, d), F32),
        compiler_params=_cparams("parallel"),
        name="proj_ln",
    )(a, w, res, g, b)


def _dn_body(proj_ref, ba_ref, bat_ref, convw_ref, gprow_ref, gpcol_ref, normw_ref,
             o_ref, s_ref, halo_ref):
    rows = proj_ref.shape[0]
    hn, dk, ck = DN_HEADS, DN_DK, DN_CHUNK
    n_chunks = rows // ck
    n_qkv = 3 * hn * dk

    @pl.when(pl.program_id(1) == 0)
    def _():
        s_ref[...] = jnp.zeros_like(s_ref)
        halo_ref[...] = jnp.zeros_like(halo_ref)

    ba = ba_ref[...]
    beta_all = 1.0 / (1.0 + jnp.exp(-ba))
    gprow = gprow_ref[...]
    g_all = -jnp.exp(gprow[0:1]) * _softplus(ba + gprow[1:2])
    row_in_chunk = lax.broadcasted_iota(I32, g_all.shape, 0) & (ck - 1)
    gcum = g_all
    step = 1
    while step < ck:
        gcum = gcum + jnp.where(row_in_chunk >= step, pltpu.roll(gcum, step, 0), 0.0)
        step *= 2
    gpcol = gpcol_ref[...]
    gt = -jnp.exp(gpcol[:, 0:1]) * _softplus(bat_ref[hn:2 * hn, :] + gpcol[:, 1:2])
    lane_in_chunk = lax.broadcasted_iota(I32, gt.shape, 1) & (ck - 1)
    gtcum = gt
    step = 1
    while step < ck:
        gtcum = gtcum + jnp.where(lane_in_chunk >= step, pltpu.roll(gtcum, step, 1), 0.0)
        step *= 2

    ri = lax.broadcasted_iota(I32, (rows, rows), 0)
    ci = lax.broadcasted_iota(I32, (rows, rows), 1)
    strict = ((ri // ck) == (ci // ck)) & (ri > ci)
    eye = ri == ci
    row8 = lax.broadcasted_iota(I32, (8, dk), 0)

    def conv_silu(col0):
        x = proj_ref[:, col0:col0 + dk]
        w = convw_ref[:, col0:col0 + dk]
        prev = halo_ref[:, col0:col0 + dk]
        y = x * w[DN_CONV - 1:DN_CONV]
        fix = jnp.zeros((8, dk), F32)
        for j in range(1, DN_CONV):
            wj = w[DN_CONV - 1 - j:DN_CONV - j]
            xj = pltpu.roll(x, j, 0)
            y = y + xj * wj
            fix = fix + jnp.where(row8 < j, (pltpu.roll(prev, j, 0) - xj[0:8]) * wj, 0.0)
        y = jnp.concatenate([y[0:8] + fix, y[8:]], axis=0)
        return _silu(y)

    for h in range(hn):
        gc = gcum[:, hn + h:hn + h + 1]
        gr = gtcum[h:h + 1, :]
        dec = jnp.exp(jnp.where(strict, gc - gr, -jnp.inf))
        bcol = beta_all[:, h:h + 1]
        eg = jnp.exp(gc)
        glast = jnp.concatenate(
            [jnp.broadcast_to(gc[c * ck + ck - 1:c * ck + ck], (ck, 1)) for c in range(n_chunks)],
            axis=0)

        q = conv_silu(h * dk)
        k = conv_silu(hn * dk + h * dk)
        v = conv_silu(2 * hn * dk + h * dk)
        q = q * lax.rsqrt(jnp.sum(q * q, axis=-1, keepdims=True) + RMS_EPS) * (dk ** -0.5)
        k = k * lax.rsqrt(jnp.sum(k * k, axis=-1, keepdims=True) + RMS_EPS)
        kb = k * bcol
        vb = v * bcol

        gram = _dot_nt(jnp.concatenate([kb, q], axis=0).astype(BF16), k.astype(BF16))
        a_neg = -(gram[0:rows] * dec)
        aqk = gram[rows:] * jnp.where(eye, 1.0, dec)
        inv = jnp.where(eye, 1.0, a_neg)
        apow = a_neg
        for _ in range(5):
            ab = apow.astype(BF16)
            apow = _dot(ab, ab)
            inv = inv + _dot(inv.astype(BF16), apow.astype(BF16))
        uw = _dot(inv.astype(BF16), jnp.concatenate([vb, kb * eg], axis=1).astype(BF16))
        u = uw[:, 0:DN_DV]
        w_ = uw[:, DN_DV:]
        qg = q * eg
        kdec = k * jnp.exp(glast - gc)

        s = s_ref[h]
        vnew, qs = [], []
        for c in range(n_chunks):
            r0 = c * ck
            lhs = jnp.concatenate([w_[r0:r0 + ck], qg[r0:r0 + ck]], axis=0).astype(BF16)
            ws_qs = _dot(lhs, s.astype(BF16))
            vn = u[r0:r0 + ck] - ws_qs[0:ck]
            vnew.append(vn)
            qs.append(ws_qs[ck:])
            gl = jnp.exp(gc[r0 + ck - 1:r0 + ck])
            s = s * gl + _dot_tn(kdec[r0:r0 + ck].astype(BF16), vn.astype(BF16))
        s_ref[h] = s
        o = jnp.concatenate(qs, axis=0) + _dot(
            aqk.astype(BF16), jnp.concatenate(vnew, axis=0).astype(BF16))

        o = o * lax.rsqrt(jnp.mean(o * o, axis=-1, keepdims=True) + RMS_EPS) * normw_ref[...]
        z = proj_ref[:, n_qkv + h * DN_DV:n_qkv + (h + 1) * DN_DV]
        o_ref[:, h * DN_DV:(h + 1) * DN_DV] = (o * _silu(z)).astype(o_ref.dtype)

    halo_ref[...] = proj_ref[rows - 8:rows, 0:n_qkv]


def _deltanet_core(proj, ba, bat, conv_w, gp_row, gp_col, norm_w, batch, seq):
    n = proj.shape[0]
    r = DN_ROWS
    nt = seq // r
    n_qkv = 3 * DN_HEADS * DN_DK
    return pl.pallas_call(
        _dn_body,
        grid=(batch, nt),
        in_specs=[
            pl.BlockSpec((r, proj.shape[1]), lambda b, t: (b * nt + t, 0)),
            pl.BlockSpec((r, V7X_LANES), lambda b, t: (b * nt + t, 0)),
            pl.BlockSpec((2 * DN_HEADS, r), lambda b, t: (0, b * nt + t)),
            pl.BlockSpec((DN_CONV, n_qkv), lambda b, t: (0, 0)),
            pl.BlockSpec((2, V7X_LANES), lambda b, t: (0, 0)),
            pl.BlockSpec((DN_HEADS, 2), lambda b, t: (0, 0)),
            pl.BlockSpec((1, DN_DV), lambda b, t: (0, 0)),
        ],
        out_specs=pl.BlockSpec((r, DN_HEADS * DN_DV), lambda b, t: (b * nt + t, 0)),
        out_shape=jax.ShapeDtypeStruct((n, DN_HEADS * DN_DV), BF16),
        scratch_shapes=[pltpu.VMEM((DN_HEADS, DN_DK, DN_DV), F32),
                        pltpu.VMEM((8, n_qkv), F32)],
        compiler_params=_cparams("arbitrary", "arbitrary"),
        name="deltanet",
    )(proj, ba, bat, conv_w, gp_row, gp_col, norm_w)


def _swa_body(sink_ref, q_ref, kc_ref, kp_ref, vc_ref, vp_ref, o_ref):
    w, hd = SW_BLOCK, SW_HEAD_DIM
    group = SW_Q_HEADS // SW_KV_HEADS
    qi = lax.broadcasted_iota(I32, (w, 2 * w), 0)
    sj = lax.broadcasted_iota(I32, (w, 2 * w), 1)
    dist = qi + w - sj
    has_prev = pl.program_id(1) > 0
    valid = (dist >= 0) & (dist < w) & ((sj >= w) | has_prev)
    distf = dist.astype(F32)
    for kh in range(SW_KV_HEADS):
        cs = slice(kh * hd, (kh + 1) * hd)
        kband = jnp.concatenate([kp_ref[:, cs], kc_ref[:, cs]], axis=0).astype(BF16)
        vband = jnp.concatenate([vp_ref[:, cs], vc_ref[:, cs]], axis=0).astype(BF16)
        for g in range(group):
            hq = kh * group + g
            slope = 2.0 ** (-8.0 * (hq + 1) / SW_Q_HEADS)
            qh = (q_ref[:, hq * hd:(hq + 1) * hd] * (hd ** -0.5)).astype(BF16)
            s = _dot_nt(qh, kband) - slope * distf
            s = jnp.where(valid, s, -jnp.inf)
            sink = sink_ref[hq]
            m = jnp.maximum(jnp.max(s, axis=-1, keepdims=True), sink)
            p = jnp.exp(s - m)
            den = jnp.sum(p, axis=-1, keepdims=True) + jnp.exp(sink - m)
            o = _dot(p.astype(BF16), vband) / den
            o_ref[:, hq * hd:(hq + 1) * hd] = o.astype(o_ref.dtype)


def _swa(qkv, sinks, batch, seq):
    n = qkv.shape[0]
    w = SW_BLOCK
    nb = seq // w
    nq = SW_Q_HEADS * SW_HEAD_DIM
    nkv = SW_KV_HEADS * SW_HEAD_DIM
    kcol = nq // nkv
    cur = lambda b, i: b * nb + i
    prev = lambda b, i: b * nb + jnp.maximum(i - 1, 0)
    return pl.pallas_call(
        _swa_body,
        grid=(batch, nb),
        in_specs=[
            pl.BlockSpec(memory_space=pltpu.SMEM),
            pl.BlockSpec((w, nq), lambda b, i: (cur(b, i), 0)),
            pl.BlockSpec((w, nkv), lambda b, i: (cur(b, i), kcol)),
            pl.BlockSpec((w, nkv), lambda b, i: (prev(b, i), kcol)),
            pl.BlockSpec((w, nkv), lambda b, i: (cur(b, i), kcol + 1)),
            pl.BlockSpec((w, nkv), lambda b, i: (prev(b, i), kcol + 1)),
        ],
        out_specs=pl.BlockSpec((w, nq), lambda b, i: (cur(b, i), 0)),
        out_shape=jax.ShapeDtypeStruct((n, nq), BF16),
        compiler_params=_cparams("parallel", "parallel"),
        name="swa",
    )(sinks, qkv, qkv, qkv, qkv, qkv)


def _ffn_tile(x_ref, wgu_ref, wd_ref):
    xb = x_ref[...].astype(BF16)
    d_ff = wd_ref.shape[1]
    y = None
    for c0, cw in FF_CHUNKS:
        hg = _dot(xb, wgu_ref[0, :, c0:c0 + cw])
        hu = _dot(xb, wgu_ref[0, :, d_ff + c0:d_ff + c0 + cw])
        act = (_silu(hg) * hu).astype(BF16)
        part = _dot(act, wd_ref[0, c0:c0 + cw, :])
        y = part if y is None else y + part
    return y


def _ffn_group_body(eid_ref, nt_ref, x_ref, wgu_ref, wd_ref, o_ref):
    used = pl.program_id(0) < nt_ref[0]

    @pl.when(used)
    def _():
        o_ref[...] = _ffn_tile(x_ref, wgu_ref, wd_ref).astype(o_ref.dtype)

    @pl.when(jnp.logical_not(used))
    def _():
        o_ref[...] = jnp.zeros_like(o_ref)


def _ffn_grouped(xs, w_gu, w_down, eid, nt):
    r, d = xs.shape
    tm = ROW_TILE
    d_ff = w_down.shape[1]
    assert sum(cw for _, cw in FF_CHUNKS) == d_ff
    tile = lambda i, eid_ref, nt_ref: jnp.minimum(i, nt_ref[0] - 1)
    return pl.pallas_call(
        _ffn_group_body,
        grid_spec=pltpu.PrefetchScalarGridSpec(
            num_scalar_prefetch=2,
            grid=(r // tm,),
            in_specs=[
                pl.BlockSpec((tm, d), lambda i, e, n: (tile(i, e, n), 0)),
                pl.BlockSpec((1, d, 2 * d_ff), lambda i, e, n: (e[tile(i, e, n)], 0, 0)),
                pl.BlockSpec((1, d_ff, d), lambda i, e, n: (e[tile(i, e, n)], 0, 0)),
            ],
            out_specs=pl.BlockSpec((tm, d), lambda i, e, n: (i, 0)),
        ),
        out_shape=jax.ShapeDtypeStruct((r, d), BF16),
        compiler_params=_cparams("arbitrary"),
        name="ffn_grouped",
    )(eid, nt, xs, w_gu, w_down)


def _ffn_ln_body(x_ref, wgu_ref, wd_ref, g_ref, b_ref, o_ref):
    y = _ffn_tile(x_ref, wgu_ref, wd_ref)
    o_ref[...] = _layer_norm(DEEP_ALPHA * x_ref[...] + y, g_ref[...], b_ref[...])


def _ffn_dense_ln(x, w_gu, w_down, g, b):
    n, d = x.shape
    tm = ROW_TILE
    d_ff = w_down.shape[1]
    assert sum(cw for _, cw in FF_CHUNKS) == d_ff
    return pl.pallas_call(
        _ffn_ln_body,
        grid=(n // tm,),
        in_specs=[pl.BlockSpec((tm, d), lambda i: (i, 0)),
                  pl.BlockSpec((1, d, 2 * d_ff), lambda i: (0, 0, 0)),
                  pl.BlockSpec((1, d_ff, d), lambda i: (0, 0, 0)),
                  pl.BlockSpec((1, d), lambda i: (0, 0)),
                  pl.BlockSpec((1, d), lambda i: (0, 0))],
        out_specs=pl.BlockSpec((tm, d), lambda i: (i, 0)),
        out_shape=jax.ShapeDtypeStruct((n, d), F32),
        compiler_params=_cparams("parallel"),
        name="ffn_dense_ln",
    )(x, w_gu, w_down, g, b)


def _split3(a):
    hi = a.astype(BF16)
    r1 = a - hi.astype(F32)
    mid = r1.astype(BF16)
    lo = (r1 - mid.astype(F32)).astype(BF16)
    return hi, mid, lo


def _router_body(x_ref, w_ref, post_ref, poscol_ref, gatecol_ref, cnt_ref):
    tm = x_ref.shape[0]
    ne = N_EXPERTS
    xs = _split3(x_ref[...])
    ws = _split3(w_ref[...])
    logits = None
    for order in (2, 1, 0):
        for i in range(order + 1):
            term = _dot_nt(ws[i], xs[order - i])
            logits = term if logits is None else logits + term

    eidx = lax.broadcasted_iota(I32, (ne, tm), 0)
    m1 = jnp.max(logits, axis=0, keepdims=True)
    i1 = jnp.min(jnp.where(logits == m1, eidx, ne), axis=0, keepdims=True)
    mask1 = eidx == i1
    rest = jnp.where(mask1, -jnp.inf, logits)
    m2 = jnp.max(rest, axis=0, keepdims=True)
    i2 = jnp.min(jnp.where(rest == m2, eidx, ne), axis=0, keepdims=True)
    mask2 = eidx == i2
    ex = jnp.exp(m2 - m1)
    w1 = 1.0 / (1.0 + ex)
    w2 = ex / (1.0 + ex)
    gates = jnp.where(mask1, w1, jnp.where(mask2, w2, 0.0))
    sel = mask1 | mask2

    ti = lax.broadcasted_iota(I32, (tm, tm), 0)
    tj = lax.broadcasted_iota(I32, (tm, tm), 1)
    before = jnp.where(ti < tj, 1.0, 0.0).astype(BF16)
    self_f = jnp.where(sel, 1.0, 0.0)
    rank = _dot(self_f.astype(BF16), before)
    pos = jnp.where(sel, rank.astype(I32), -1)
    post_ref[...] = pos
    cnt = jnp.sum(self_f, axis=1, keepdims=True).astype(I32)
    cnt_ref[0] = jnp.broadcast_to(cnt, (ne, V7X_LANES))

    p1 = pos + 1
    pieces = [(p1 >> 8).astype(F32).astype(BF16), (p1 & 255).astype(F32).astype(BF16)]
    pieces += list(_split3(gates))
    ident = jnp.where(ti == tj, 1.0, 0.0).astype(BF16)
    cols = _dot_nt(ident, jnp.concatenate(pieces, axis=0))
    poscol_ref[...] = (cols[:, 0:ne] * 256.0 + cols[:, ne:2 * ne]).astype(I32) - 1
    gatecol_ref[...] = cols[:, 2 * ne:3 * ne] + (cols[:, 3 * ne:4 * ne] + cols[:, 4 * ne:5 * ne])


def _router(x, w_router_t):
    n, d = x.shape
    tm = MOE_TILE
    ne = N_EXPERTS
    nt = n // tm
    return pl.pallas_call(
        _router_body,
        grid=(nt,),
        in_specs=[pl.BlockSpec((tm, d), lambda i: (i, 0)),
                  pl.BlockSpec((ne, d), lambda i: (0, 0))],
        out_specs=[pl.BlockSpec((ne, tm), lambda i: (0, i)),
                   pl.BlockSpec((tm, ne), lambda i: (i, 0)),
                   pl.BlockSpec((tm, ne), lambda i: (i, 0)),
                   pl.BlockSpec((1, ne, V7X_LANES), lambda i: (i, 0, 0))],
        out_shape=[jax.ShapeDtypeStruct((ne, n), I32),
                   jax.ShapeDtypeStruct((n, ne), I32),
                   jax.ShapeDtypeStruct((n, ne), F32),
                   jax.ShapeDtypeStruct((nt, ne, V7X_LANES), I32)],
        compiler_params=_cparams("parallel"),
        name="router",
    )(x, w_router_t)


def _piece_copies(rows, make_copy):
    conds, copies = [], []
    for slot, size in enumerate(COPY_SIZES):
        offset = pl.multiple_of(rows & ~(2 * size - 1), MOE_ALIGN)
        conds.append((rows & size) != 0)
        copies.append(make_copy(offset, size, slot))
    for cond, cp in zip(conds, copies):
        pl.when(cond)(cp.start)
    for cond, cp in zip(conds, copies):
        pl.when(cond)(cp.wait)


def _dispatch_body(off_ref, pc_ref, zoff_ref, zlen_ref, x_ref, post_ref, xs_hbm, buf, sem):
    t = pl.program_id(0)
    tm = x_ref.shape[0]
    xb = x_ref[...].astype(BF16)
    for e in range(N_EXPERTS):
        seg_off = off_ref[t * N_EXPERTS + e]
        seg_rows = pc_ref[t * N_EXPERTS + e]
        posrow = post_ref[e:e + 1, :]

        def chunk(kc, carry, seg_off=seg_off, seg_rows=seg_rows, posrow=posrow):
            r = lax.broadcasted_iota(I32, (MOE_CHUNK, tm), 0) + kc * MOE_CHUNK
            sel = jnp.where(posrow == r, 1.0, 0.0).astype(BF16)
            buf[...] = _dot(sel, xb).astype(BF16)
            rows = jnp.minimum(seg_rows - kc * MOE_CHUNK, MOE_CHUNK)
            base = pl.multiple_of(seg_off + kc * MOE_CHUNK, MOE_ALIGN)

            def mk(offset, size, slot):
                return pltpu.make_async_copy(
                    buf.at[pl.ds(offset, size)],
                    xs_hbm.at[pl.ds(pl.multiple_of(base + offset, MOE_ALIGN), size)],
                    sem.at[slot])
            _piece_copies(rows, mk)
            return carry

        lax.fori_loop(0, (seg_rows + MOE_CHUNK - 1) // MOE_CHUNK, chunk, 0)

    @pl.when(t == pl.num_programs(0) - 1)
    def _():
        buf[...] = jnp.zeros_like(buf)
        for e in range(N_EXPERTS):
            zoff = zoff_ref[e]
            zlen = zlen_ref[e]
            full = zlen >> 8

            def zchunk(kc, carry, zoff=zoff):
                cp = pltpu.make_async_copy(
                    buf, xs_hbm.at[pl.ds(pl.multiple_of(zoff + kc * MOE_CHUNK, MOE_ALIGN), MOE_CHUNK)],
                    sem.at[0])
                cp.start()
                cp.wait()
                return carry
            lax.fori_loop(0, full, zchunk, 0)
            base = pl.multiple_of(zoff + full * MOE_CHUNK, MOE_ALIGN)

            def mk(offset, size, slot, base=base):
                return pltpu.make_async_copy(
                    buf.at[pl.ds(offset, size)],
                    xs_hbm.at[pl.ds(pl.multiple_of(base + offset, MOE_ALIGN), size)],
                    sem.at[slot])
            _piece_copies(zlen & (MOE_CHUNK - 1), mk)

        used_rows = zoff_ref[N_EXPERTS - 1] + zlen_ref[N_EXPERTS - 1]

        def tail(kc, carry):
            cp = pltpu.make_async_copy(
                buf, xs_hbm.at[pl.ds(pl.multiple_of(used_rows + kc * MOE_CHUNK, MOE_ALIGN), MOE_CHUNK)],
                sem.at[0])
            cp.start()
            cp.wait()
            return carry
        lax.fori_loop(0, (xs_hbm.shape[0] - used_rows) // MOE_CHUNK, tail, 0)


def _dispatch(x, pos_t, off, pc, zoff, zlen, n_rows):
    n, d = x.shape
    tm = MOE_TILE
    return pl.pallas_call(
        _dispatch_body,
        grid_spec=pltpu.PrefetchScalarGridSpec(
            num_scalar_prefetch=4,
            grid=(n // tm,),
            in_specs=[pl.BlockSpec((tm, d), lambda i, *_: (i, 0)),
                      pl.BlockSpec((N_EXPERTS, tm), lambda i, *_: (0, i))],
            out_specs=pl.BlockSpec(memory_space=pl.ANY),
            scratch_shapes=[pltpu.VMEM((MOE_CHUNK, d), BF16),
                            pltpu.SemaphoreType.DMA((len(COPY_SIZES),))],
        ),
        out_shape=jax.ShapeDtypeStruct((n_rows, d), BF16),
        compiler_params=_cparams("arbitrary"),
        name="moe_dispatch",
    )(off, pc, zoff, zlen, x, pos_t)


def _combine_body(off_ref, pc_ref, x_ref, poscol_ref, gatecol_ref, g_ref, b_ref, ys_hbm,
                  o_ref, ybuf, acc, sem):
    t = pl.program_id(0)
    tm = x_ref.shape[0]

    @pl.when(t == 0)
    def _():
        ybuf[...] = jnp.zeros_like(ybuf)

    acc[...] = jnp.zeros_like(acc)
    for e in range(N_EXPERTS):
        seg_off = off_ref[t * N_EXPERTS + e]
        seg_rows = pc_ref[t * N_EXPERTS + e]
        pcol = poscol_ref[:, e:e + 1]
        gcol = gatecol_ref[:, e:e + 1]

        def chunk(kc, carry, seg_off=seg_off, seg_rows=seg_rows, pcol=pcol, gcol=gcol):
            rows = jnp.minimum(seg_rows - kc * MOE_CHUNK, MOE_CHUNK)
            base = pl.multiple_of(seg_off + kc * MOE_CHUNK, MOE_ALIGN)

            def mk(offset, size, slot):
                return pltpu.make_async_copy(
                    ys_hbm.at[pl.ds(pl.multiple_of(base + offset, MOE_ALIGN), size)],
                    ybuf.at[pl.ds(offset, size)],
                    sem.at[slot])
            _piece_copies(rows, mk)
            lane = lax.broadcasted_iota(I32, (tm, MOE_CHUNK), 1) + kc * MOE_CHUNK
            sel_t = jnp.where(pcol == lane, 1.0, 0.0).astype(BF16)
            acc[...] += gcol * _dot(sel_t, ybuf[...])
            return carry

        lax.fori_loop(0, (seg_rows + MOE_CHUNK - 1) // MOE_CHUNK, chunk, 0)

    o_ref[...] = _layer_norm(DEEP_ALPHA * x_ref[...] + acc[...], g_ref[...], b_ref[...])


def _combine_ln(x, pos_col, gate_col, g, b, ys, off, pc):
    n, d = x.shape
    tm = MOE_TILE
    ne = N_EXPERTS
    return pl.pallas_call(
        _combine_body,
        grid_spec=pltpu.PrefetchScalarGridSpec(
            num_scalar_prefetch=2,
            grid=(n // tm,),
            in_specs=[pl.BlockSpec((tm, d), lambda i, *_: (i, 0)),
                      pl.BlockSpec((tm, ne), lambda i, *_: (i, 0)),
                      pl.BlockSpec((tm, ne), lambda i, *_: (i, 0)),
                      pl.BlockSpec((1, d), lambda i, *_: (0, 0)),
                      pl.BlockSpec((1, d), lambda i, *_: (0, 0)),
                      pl.BlockSpec(memory_space=pl.ANY)],
            out_specs=pl.BlockSpec((tm, d), lambda i, *_: (i, 0)),
            scratch_shapes=[pltpu.VMEM((MOE_CHUNK, d), BF16),
                            pltpu.VMEM((tm, d), F32),
                            pltpu.SemaphoreType.DMA((len(COPY_SIZES),))],
        ),
        out_shape=jax.ShapeDtypeStruct((n, d), F32),
        compiler_params=_cparams("arbitrary"),
        name="moe_combine_ln",
    )(off, pc, x, pos_col, gate_col, g, b, ys)


def _moe_plan(cnt, n_tiles_max):
    tmf = ROW_TILE
    pc = (cnt + (MOE_ALIGN - 1)) // MOE_ALIGN * MOE_ALIGN
    seg = jnp.sum(pc, axis=0)
    reg = (seg + (tmf - 1)) // tmf * tmf
    base = jnp.cumsum(reg) - reg
    off = base[None, :] + jnp.cumsum(pc, axis=0) - pc
    tile_end = jnp.cumsum(reg // tmf)
    nt = tile_end[-1:]
    tiles = jnp.arange(n_tiles_max, dtype=I32)
    eid = jnp.minimum(jnp.sum((tile_end[None, :] <= tiles[:, None]).astype(I32), axis=1),
                      N_EXPERTS - 1)
    return (off.reshape(-1).astype(I32), pc.reshape(-1).astype(I32),
            (base + seg).astype(I32), (reg - seg).astype(I32), eid.astype(I32), nt.astype(I32))


def _moe_ln(x, w_router, w_gu, w_down, g, b):
    n, d = x.shape
    n_tok_tiles = n // MOE_TILE
    pos_t, pos_col, gate_col, cnt = _router(x, w_router.T)
    max_rows = 2 * n + (MOE_ALIGN - 1) * N_EXPERTS * n_tok_tiles + N_EXPERTS * (ROW_TILE - MOE_ALIGN)
    n_tiles_max = -(-max_rows // ROW_TILE)
    off, pc, zoff, zlen, eid, nt = _moe_plan(cnt[:, :, 0], n_tiles_max)
    xs = _dispatch(x, pos_t, off, pc, zoff, zlen, n_tiles_max * ROW_TILE)
    ys = _ffn_grouped(xs, w_gu, w_down, eid, nt)
    return _combine_ln(x, pos_col, gate_col, g, b, ys, off, pc)


def kernel(x, dn_w_in, dn_conv_w, dn_a_log, dn_dt_bias, dn_norm_w, dn_w_out, w_kv_shared,
           sw_w_q, sw_sinks, sw_w_out, ffn_w_gu, ffn_w_down, moe_w_router, moe_w_gu,
           moe_w_down, ln_g, ln_b):
    batch, seq, d = x.shape
    n = batch * seq
    hn = DN_HEADS
    assert seq % DN_ROWS == 0 and n % ROW_TILE == 0 and n % MOE_TILE == 0
    assert dn_w_in.shape[0] == 1 and sw_w_q.shape[0] == 1
    xf = x.reshape(n, d)
    row = lambda v: v.reshape(1, -1).astype(F32)

    n_main = 4 * hn * DN_DK
    w_in = dn_w_in[0]
    w_main = w_in[:, :n_main].astype(BF16)
    w_ba = jnp.pad(w_in[:, n_main:], ((0, 0), (0, V7X_LANES - 2 * hn))).astype(BF16)
    w_bat = w_in[:, n_main:].T.astype(BF16)
    proj, ba, bat = _dn_inproj(xf, w_main, w_ba, w_bat)
    gate_params = jnp.stack([dn_a_log[0], dn_dt_bias[0]]).astype(F32)
    gp_row = jnp.pad(gate_params, ((0, 0), (hn, V7X_LANES - 2 * hn)))
    o = _deltanet_core(proj, ba, bat, dn_conv_w[0].astype(F32), gp_row, gate_params.T,
                       row(dn_norm_w[0]), batch, seq)
    x1 = _proj_ln(o, dn_w_out[0].astype(BF16), xf, row(ln_g[0, 0]), row(ln_b[0, 0]))
    x2 = _ffn_dense_ln(x1, ffn_w_gu.astype(BF16), ffn_w_down.astype(BF16),
                       row(ln_g[0, 1]), row(ln_b[0, 1]))

    w_qkv = jnp.concatenate([sw_w_q[0], w_kv_shared], axis=1).astype(BF16)
    qkv = _proj(x2, w_qkv)
    att = _swa(qkv, sw_sinks[0].astype(F32), batch, seq)
    x3 = _proj_ln(att, sw_w_out[0].astype(BF16), x2, row(ln_g[1, 0]), row(ln_b[1, 0]))
    x4 = _moe_ln(x3, moe_w_router[0].astype(F32), moe_w_gu[0].astype(BF16),
                 moe_w_down[0].astype(BF16), row(ln_g[1, 1]), row(ln_b[1, 1]))
    return x4.reshape(batch, seq, d)
```

```python
import functools

import jax
import jax.numpy as jnp
from jax import lax
from jax.experimental import pallas as pl
from jax.experimental.pallas import tpu as pltpu

F32 = jnp.float32
BF16 = jnp.bfloat16
I32 = jnp.int32

DEPTH = 2
DN_HEADS = 8
DN_DK = 128
DN_DV = 128
DN_CONV = 4
DN_CHUNK = 64
SW_Q_HEADS = 16
SW_KV_HEADS = 4
SW_HEAD_DIM = 64
SW_BLOCK = 128
N_EXPERTS = 8
TOP_K = 2
DEEP_ALPHA = (2 * DEPTH) ** 0.25
LN_EPS = 1e-5
RMS_EPS = 1e-6

V7X_VMEM_BYTES = 64 * 1024 * 1024
V7X_LANES = 128
V7X_BF16_SUBLANES = 16

VMEM_LIMIT = V7X_VMEM_BYTES - 8 * 1024 * 1024
ROW_TILE = 512
DN_ROWS = 256
MOE_TILE = 512
MOE_ALIGN = V7X_BF16_SUBLANES
MOE_PACK = TOP_K * MOE_TILE + N_EXPERTS * MOE_ALIGN
MOE_ZERO_ROWS = 256
COPY_SIZES = (512, 256, 128, 64, 32, 16)
FF_CHUNKS = ((0, 1536), (1536, 1280))


def _cparams(*sem):
    return pltpu.CompilerParams(dimension_semantics=sem, vmem_limit_bytes=VMEM_LIMIT)


def _dot(a, b):
    return jnp.dot(a, b, preferred_element_type=F32)


def _dot_nt(a, b):
    return lax.dot_general(a, b, (((1,), (1,)), ((), ())), preferred_element_type=F32)


def _dot_tn(a, b):
    return lax.dot_general(a, b, (((0,), (0,)), ((), ())), preferred_element_type=F32)


def _silu(x):
    return x * (1.0 / (1.0 + jnp.exp(-x)))


def _softplus(x):
    return jnp.maximum(x, 0.0) + jnp.log1p(jnp.exp(-jnp.abs(x)))


def _layer_norm(y, g, b):
    mu = jnp.mean(y, axis=-1, keepdims=True)
    yc = y - mu
    var = jnp.mean(yc * yc, axis=-1, keepdims=True)
    return yc * lax.rsqrt(var + LN_EPS) * g + b


def _dn_inproj_body(x_ref, w_ref, wba_ref, wbat_ref, proj_ref, ba_ref, bat_ref):
    xb = x_ref[...].astype(BF16)
    n_out = w_ref.shape[1]
    for c0 in range(0, n_out, 1024):
        proj_ref[:, c0:c0 + 1024] = _dot(xb, w_ref[:, c0:c0 + 1024])
    ba_ref[...] = _dot(xb, wba_ref[...])
    bat_ref[...] = _dot_nt(wbat_ref[...], xb)


def _dn_inproj(x, w_main, w_ba, w_bat):
    n, d = x.shape
    n_out = w_main.shape[1]
    tm = ROW_TILE
    return pl.pallas_call(
        _dn_inproj_body,
        grid=(n // tm,),
        in_specs=[
            pl.BlockSpec((tm, d), lambda i: (i, 0)),
            pl.BlockSpec((d, n_out), lambda i: (0, 0)),
            pl.BlockSpec((d, V7X_LANES), lambda i: (0, 0)),
            pl.BlockSpec((2 * DN_HEADS, d), lambda i: (0, 0)),
        ],
        out_specs=[
            pl.BlockSpec((tm, n_out), lambda i: (i, 0)),
            pl.BlockSpec((tm, V7X_LANES), lambda i: (i, 0)),
            pl.BlockSpec((2 * DN_HEADS, tm), lambda i: (0, i)),
        ],
        out_shape=[
            jax.ShapeDtypeStruct((n, n_out), F32),
            jax.ShapeDtypeStruct((n, V7X_LANES), F32),
            jax.ShapeDtypeStruct((2 * DN_HEADS, n), F32),
        ],
        compiler_params=_cparams("parallel"),
        name="dn_inproj",
    )(x, w_main, w_ba, w_bat)


def _proj_body(x_ref, w_ref, o_ref):
    o_ref[...] = _dot(x_ref[...].astype(BF16), w_ref[...])


def _proj(x, w):
    n, k = x.shape
    m = w.shape[1]
    tm = ROW_TILE
    return pl.pallas_call(
        _proj_body,
        grid=(n // tm,),
        in_specs=[pl.BlockSpec((tm, k), lambda i: (i, 0)),
                  pl.BlockSpec((k, m), lambda i: (0, 0))],
        out_specs=pl.BlockSpec((tm, m), lambda i: (i, 0)),
        out_shape=jax.ShapeDtypeStruct((n, m), F32),
        compiler_params=_cparams("parallel"),
        name="proj",
    )(x, w)


def _proj_ln_body(a_ref, w_ref, res_ref, g_ref, b_ref, o_ref):
    h = _dot(a_ref[...].astype(BF16), w_ref[...])
    o_ref[...] = _layer_norm(DEEP_ALPHA * res_ref[...] + h, g_ref[...], b_ref[...])


def _proj_ln(a, w, res, g, b):
    n, k = a.shape
    d = w.shape[1]
    tm = ROW_TILE
    return pl.pallas_call(
        _proj_ln_body,
        grid=(n // tm,),
        in_specs=[pl.BlockSpec((tm, k), lambda i: (i, 0)),
                  pl.BlockSpec((k, d), lambda i: (0, 0)),
                  pl.BlockSpec((tm, d), lambda i: (i, 0)),
                  pl.BlockSpec((1, d), lambda i: (0, 0)),
                  pl.BlockSpec((1, d), lambda i: (0, 0))],
        out_specs=pl.BlockSpec((tm, d), lambda i: (i, 0)),
        out_shape=jax.ShapeDtypeStruct((n, d), F32),
        compiler_params=_cparams("parallel"),
        name="proj_ln",
    )(a, w, res, g, b)


def _dn_body(proj_ref, ba_ref, bat_ref, convw_ref, gprow_ref, gpcol_ref, normw_ref,
             o_ref, s_ref, halo_ref):
    rows = proj_ref.shape[0]
    hn, dk, ck = DN_HEADS, DN_DK, DN_CHUNK
    n_chunks = rows // ck
    n_qkv = 3 * hn * dk

    @pl.when(pl.program_id(1) == 0)
    def _():
        s_ref[...] = jnp.zeros_like(s_ref)
        halo_ref[...] = jnp.zeros_like(halo_ref)

    ba = ba_ref[...]
    beta_all = 1.0 / (1.0 + jnp.exp(-ba))
    gprow = gprow_ref[...]
    g_all = -jnp.exp(gprow[0:1]) * _softplus(ba + gprow[1:2])
    row_in_chunk = lax.broadcasted_iota(I32, g_all.shape, 0) & (ck - 1)
    gcum = g_all
    step = 1
    while step < ck:
        gcum = gcum + jnp.where(row_in_chunk >= step, pltpu.roll(gcum, step, 0), 0.0)
        step *= 2
    gpcol = gpcol_ref[...]
    gt = -jnp.exp(gpcol[:, 0:1]) * _softplus(bat_ref[hn:2 * hn, :] + gpcol[:, 1:2])
    lane_in_chunk = lax.broadcasted_iota(I32, gt.shape, 1) & (ck - 1)
    gtcum = gt
    step = 1
    while step < ck:
        gtcum = gtcum + jnp.where(lane_in_chunk >= step, pltpu.roll(gtcum, step, 1), 0.0)
        step *= 2

    ri = lax.broadcasted_iota(I32, (rows, rows), 0)
    ci = lax.broadcasted_iota(I32, (rows, rows), 1)
    strict = ((ri // ck) == (ci // ck)) & (ri > ci)
    eye = ri == ci
    row8 = lax.broadcasted_iota(I32, (8, dk), 0)

    def conv_silu(col0):
        x = proj_ref[:, col0:col0 + dk]
        w = convw_ref[:, col0:col0 + dk]
        prev = halo_ref[:, col0:col0 + dk]
        y = x * w[DN_CONV - 1:DN_CONV]
        fix = jnp.zeros((8, dk), F32)
        for j in range(1, DN_CONV):
            wj = w[DN_CONV - 1 - j:DN_CONV - j]
            xj = pltpu.roll(x, j, 0)
            y = y + xj * wj
            fix = fix + jnp.where(row8 < j, (pltpu.roll(prev, j, 0) - xj[0:8]) * wj, 0.0)
        y = jnp.concatenate([y[0:8] + fix, y[8:]], axis=0)
        return _silu(y)

    heads = range(hn)
    gc = [gcum[:, hn + h:hn + h + 1] for h in heads]
    dec = [jnp.exp(jnp.where(strict, gc[h] - gtcum[h:h + 1, :], -jnp.inf)) for h in heads]
    eg = [jnp.exp(gc[h]) for h in heads]
    q, k, kb, vb = [], [], [], []
    for h in heads:
        bcol = beta_all[:, h:h + 1]
        qh = conv_silu(h * dk)
        kh = conv_silu(hn * dk + h * dk)
        vh = conv_silu(2 * hn * dk + h * dk)
        q.append(qh * lax.rsqrt(jnp.sum(qh * qh, axis=-1, keepdims=True) + RMS_EPS) * (dk ** -0.5))
        kh = kh * lax.rsqrt(jnp.sum(kh * kh, axis=-1, keepdims=True) + RMS_EPS)
        k.append(kh)
        kb.append(kh * bcol)
        vb.append(vh * bcol)

    gram = [_dot_nt(jnp.concatenate([kb[h], q[h]], axis=0).astype(BF16), k[h].astype(BF16))
            for h in heads]
    apow = [-(gram[h][0:rows] * dec[h]) for h in heads]
    aqk = [(gram[h][rows:] * jnp.where(eye, 1.0, dec[h])).astype(BF16) for h in heads]
    inv = [jnp.where(eye, 1.0, apow[h]) for h in heads]
    for _ in range(5):
        ab = [apow[h].astype(BF16) for h in heads]
        apow = [_dot(ab[h], ab[h]) for h in heads]
        inv = [inv[h] + _dot(inv[h].astype(BF16), apow[h].astype(BF16)) for h in heads]
    uw = [_dot(inv[h].astype(BF16),
               jnp.concatenate([vb[h], kb[h] * eg[h]], axis=1).astype(BF16)) for h in heads]
    wq = [[jnp.concatenate([uw[h][c * ck:(c + 1) * ck, DN_DV:],
                            q[h][c * ck:(c + 1) * ck] * eg[h][c * ck:(c + 1) * ck]],
                           axis=0).astype(BF16) for c in range(n_chunks)] for h in heads]
    kdec = [[(k[h][c * ck:(c + 1) * ck]
              * jnp.exp(gc[h][(c + 1) * ck - 1:(c + 1) * ck] - gc[h][c * ck:(c + 1) * ck])
              ).astype(BF16) for c in range(n_chunks)] for h in heads]

    s = [s_ref[h] for h in heads]
    vnew = [[] for _ in heads]
    qs = [[] for _ in heads]
    for c in range(n_chunks):
        r0 = c * ck
        ws_qs = [_dot(wq[h][c], s[h].astype(BF16)) for h in heads]
        for h in heads:
            vn = uw[h][r0:r0 + ck, 0:DN_DV] - ws_qs[h][0:ck]
            vnew[h].append(vn.astype(BF16))
            qs[h].append(ws_qs[h][ck:])
        upd = [_dot_tn(kdec[h][c], vnew[h][c]) for h in heads]
        s = [s[h] * jnp.exp(gc[h][r0 + ck - 1:r0 + ck]) + upd[h] for h in heads]
    for h in heads:
        s_ref[h] = s[h]
    o_all = [jnp.concatenate(qs[h], axis=0) + _dot(aqk[h], jnp.concatenate(vnew[h], axis=0))
             for h in heads]
    for h in heads:
        o = o_all[h]
        o = o * lax.rsqrt(jnp.mean(o * o, axis=-1, keepdims=True) + RMS_EPS) * normw_ref[...]
        z = proj_ref[:, n_qkv + h * DN_DV:n_qkv + (h + 1) * DN_DV]
        o_ref[:, h * DN_DV:(h + 1) * DN_DV] = (o * _silu(z)).astype(o_ref.dtype)

    halo_ref[...] = proj_ref[rows - 8:rows, 0:n_qkv]


def _deltanet_core(proj, ba, bat, conv_w, gp_row, gp_col, norm_w, batch, seq):
    n = proj.shape[0]
    r = DN_ROWS
    nt = seq // r
    n_qkv = 3 * DN_HEADS * DN_DK
    return pl.pallas_call(
        _dn_body,
        grid=(batch, nt),
        in_specs=[
            pl.BlockSpec((r, proj.shape[1]), lambda b, t: (b * nt + t, 0)),
            pl.BlockSpec((r, V7X_LANES), lambda b, t: (b * nt + t, 0)),
            pl.BlockSpec((2 * DN_HEADS, r), lambda b, t: (0, b * nt + t)),
            pl.BlockSpec((DN_CONV, n_qkv), lambda b, t: (0, 0)),
            pl.BlockSpec((2, V7X_LANES), lambda b, t: (0, 0)),
            pl.BlockSpec((DN_HEADS, 2), lambda b, t: (0, 0)),
            pl.BlockSpec((1, DN_DV), lambda b, t: (0, 0)),
        ],
        out_specs=pl.BlockSpec((r, DN_HEADS * DN_DV), lambda b, t: (b * nt + t, 0)),
        out_shape=jax.ShapeDtypeStruct((n, DN_HEADS * DN_DV), BF16),
        scratch_shapes=[pltpu.VMEM((DN_HEADS, DN_DK, DN_DV), F32),
                        pltpu.VMEM((8, n_qkv), F32)],
        compiler_params=_cparams("arbitrary", "arbitrary"),
        name="deltanet",
    )(proj, ba, bat, conv_w, gp_row, gp_col, norm_w)


def _swa_body(sink_ref, q_ref, kc_ref, kp_ref, vc_ref, vp_ref, o_ref, bias_ref):
    w, hd = SW_BLOCK, SW_HEAD_DIM
    group = SW_Q_HEADS // SW_KV_HEADS
    heads = range(SW_Q_HEADS)

    @pl.when((pl.program_id(0) == 0) & (pl.program_id(1) == 0))
    def _():
        qi = lax.broadcasted_iota(I32, (w, 2 * w), 0)
        sj = lax.broadcasted_iota(I32, (w, 2 * w), 1)
        dist = qi + w - sj
        valid = (dist >= 0) & (dist < w)
        distf = dist.astype(F32)
        for hq in heads:
            slope = 2.0 ** (-8.0 * (hq + 1) / SW_Q_HEADS)
            bias = jnp.where(valid, -(slope * distf), -jnp.inf)
            bias_ref[1, hq] = bias
            bias_ref[0, hq] = jnp.where(sj >= w, bias, -jnp.inf)

    has_prev = jnp.where(pl.program_id(1) > 0, 1, 0)
    qb = (q_ref[...] * (hd ** -0.5)).astype(BF16)
    kband, vband = [], []
    for kh in range(SW_KV_HEADS):
        cs = slice(kh * hd, (kh + 1) * hd)
        kband.append(jnp.concatenate([kp_ref[:, cs], kc_ref[:, cs]], axis=0).astype(BF16))
        vband.append(jnp.concatenate([vp_ref[:, cs], vc_ref[:, cs]], axis=0).astype(BF16))
    s = [_dot_nt(qb[:, hq * hd:(hq + 1) * hd], kband[hq // group]) + bias_ref[has_prev, hq]
         for hq in heads]
    m = [jnp.maximum(jnp.max(s[hq], axis=-1, keepdims=True), sink_ref[hq]) for hq in heads]
    p = [jnp.exp(s[hq] - m[hq]) for hq in heads]
    den = [jnp.sum(p[hq], axis=-1, keepdims=True) + jnp.exp(sink_ref[hq] - m[hq]) for hq in heads]
    o = [_dot(p[hq].astype(BF16), vband[hq // group]) / den[hq] for hq in heads]
    for hq in heads:
        o_ref[:, hq * hd:(hq + 1) * hd] = o[hq].astype(o_ref.dtype)


def _swa(qkv, sinks, batch, seq):
    n = qkv.shape[0]
    w = SW_BLOCK
    nb = seq // w
    nq = SW_Q_HEADS * SW_HEAD_DIM
    nkv = SW_KV_HEADS * SW_HEAD_DIM
    kcol = nq // nkv
    cur = lambda b, i: b * nb + i
    prev = lambda b, i: b * nb + jnp.maximum(i - 1, 0)
    return pl.pallas_call(
        _swa_body,
        grid=(batch, nb),
        in_specs=[
            pl.BlockSpec(memory_space=pltpu.SMEM),
            pl.BlockSpec((w, nq), lambda b, i: (cur(b, i), 0)),
            pl.BlockSpec((w, nkv), lambda b, i: (cur(b, i), kcol)),
            pl.BlockSpec((w, nkv), lambda b, i: (prev(b, i), kcol)),
            pl.BlockSpec((w, nkv), lambda b, i: (cur(b, i), kcol + 1)),
            pl.BlockSpec((w, nkv), lambda b, i: (prev(b, i), kcol + 1)),
        ],
        out_specs=pl.BlockSpec((w, nq), lambda b, i: (cur(b, i), 0)),
        out_shape=jax.ShapeDtypeStruct((n, nq), BF16),
        scratch_shapes=[pltpu.VMEM((2, SW_Q_HEADS, w, 2 * w), F32)],
        compiler_params=_cparams("arbitrary", "arbitrary"),
        name="swa",
    )(sinks, qkv, qkv, qkv, qkv, qkv)


def _ffn_tile(x_ref, wgu_ref, wd_ref):
    xb = x_ref[...].astype(BF16)
    d_ff = wd_ref.shape[1]
    y = None
    for c0, cw in FF_CHUNKS:
        hg = _dot(xb, wgu_ref[0, :, c0:c0 + cw])
        hu = _dot(xb, wgu_ref[0, :, d_ff + c0:d_ff + c0 + cw])
        act = (_silu(hg) * hu).astype(BF16)
        part = _dot(act, wd_ref[0, c0:c0 + cw, :])
        y = part if y is None else y + part
    return y


def _ffn_group_body(eid_ref, nt_ref, x_ref, wgu_ref, wd_ref, o_ref):
    used = pl.program_id(0) < nt_ref[0]

    @pl.when(used)
    def _():
        o_ref[...] = _ffn_tile(x_ref, wgu_ref, wd_ref).astype(o_ref.dtype)

    @pl.when(jnp.logical_not(used))
    def _():
        o_ref[...] = jnp.zeros_like(o_ref)


def _ffn_grouped(xs, w_gu, w_down, eid, nt):
    r, d = xs.shape
    tm = ROW_TILE
    d_ff = w_down.shape[1]
    assert sum(cw for _, cw in FF_CHUNKS) == d_ff
    tile = lambda i, eid_ref, nt_ref: jnp.minimum(i, nt_ref[0] - 1)
    return pl.pallas_call(
        _ffn_group_body,
        grid_spec=pltpu.PrefetchScalarGridSpec(
            num_scalar_prefetch=2,
            grid=(r // tm,),
            in_specs=[
                pl.BlockSpec((tm, d), lambda i, e, n: (tile(i, e, n), 0)),
                pl.BlockSpec((1, d, 2 * d_ff), lambda i, e, n: (e[tile(i, e, n)], 0, 0)),
                pl.BlockSpec((1, d_ff, d), lambda i, e, n: (e[tile(i, e, n)], 0, 0)),
            ],
            out_specs=pl.BlockSpec((tm, d), lambda i, e, n: (i, 0)),
        ),
        out_shape=jax.ShapeDtypeStruct((r, d), BF16),
        compiler_params=_cparams("arbitrary"),
        name="ffn_grouped",
    )(eid, nt, xs, w_gu, w_down)


def _ffn_ln_body(x_ref, wgu_ref, wd_ref, g_ref, b_ref, o_ref):
    y = _ffn_tile(x_ref, wgu_ref, wd_ref)
    o_ref[...] = _layer_norm(DEEP_ALPHA * x_ref[...] + y, g_ref[...], b_ref[...])


def _ffn_dense_ln(x, w_gu, w_down, g, b):
    n, d = x.shape
    tm = ROW_TILE
    d_ff = w_down.shape[1]
    assert sum(cw for _, cw in FF_CHUNKS) == d_ff
    return pl.pallas_call(
        _ffn_ln_body,
        grid=(n // tm,),
        in_specs=[pl.BlockSpec((tm, d), lambda i: (i, 0)),
                  pl.BlockSpec((1, d, 2 * d_ff), lambda i: (0, 0, 0)),
                  pl.BlockSpec((1, d_ff, d), lambda i: (0, 0, 0)),
                  pl.BlockSpec((1, d), lambda i: (0, 0)),
                  pl.BlockSpec((1, d), lambda i: (0, 0))],
        out_specs=pl.BlockSpec((tm, d), lambda i: (i, 0)),
        out_shape=jax.ShapeDtypeStruct((n, d), F32),
        compiler_params=_cparams("parallel"),
        name="ffn_dense_ln",
    )(x, w_gu, w_down, g, b)


def _split3(a):
    hi = a.astype(BF16)
    r1 = a - hi.astype(F32)
    mid = r1.astype(BF16)
    lo = (r1 - mid.astype(F32)).astype(BF16)
    return hi, mid, lo


def _router_body(x_ref, w_ref, dest_t_ref, dest_ref, gate_ref, cnt_ref):
    tm = x_ref.shape[0]
    ne = N_EXPERTS
    xs = _split3(x_ref[...])
    ws = _split3(w_ref[...])
    logits = None
    for order in (2, 1, 0):
        for i in range(order + 1):
            term = _dot_nt(ws[i], xs[order - i])
            logits = term if logits is None else logits + term

    eidx = lax.broadcasted_iota(I32, (ne, tm), 0)
    m1 = jnp.max(logits, axis=0, keepdims=True)
    i1 = jnp.min(jnp.where(logits == m1, eidx, ne), axis=0, keepdims=True)
    mask1 = eidx == i1
    rest = jnp.where(mask1, -jnp.inf, logits)
    m2 = jnp.max(rest, axis=0, keepdims=True)
    i2 = jnp.min(jnp.where(rest == m2, eidx, ne), axis=0, keepdims=True)
    mask2 = eidx == i2
    ex = jnp.exp(m2 - m1)
    w1 = 1.0 / (1.0 + ex)
    w2 = ex / (1.0 + ex)
    sel = mask1 | mask2

    ti = lax.broadcasted_iota(I32, (tm, tm), 0)
    tj = lax.broadcasted_iota(I32, (tm, tm), 1)
    before = jnp.where(ti < tj, 1.0, 0.0).astype(BF16)
    self_f = jnp.where(sel, 1.0, 0.0)
    rank = _dot(self_f.astype(BF16), before)
    cnt = jnp.sum(self_f, axis=1, keepdims=True).astype(I32)
    cnt_ref[0] = jnp.broadcast_to(cnt, (ne, V7X_LANES))

    seg_rows = (((cnt + (MOE_ALIGN - 1)) // MOE_ALIGN) * MOE_ALIGN).astype(F32)
    seg_rows = jnp.broadcast_to(seg_rows, (ne, V7X_LANES))
    e8 = lax.broadcasted_iota(I32, (ne, V7X_LANES), 0)
    seg0 = jnp.zeros((ne, V7X_LANES), F32)
    for k in range(1, ne):
        seg0 = seg0 + jnp.where(e8 >= k, pltpu.roll(seg_rows, k, 0), 0.0)
    dest = seg0[:, 0:1] + rank
    d1 = jnp.sum(jnp.where(mask1, dest, 0.0), axis=0, keepdims=True)
    d2 = jnp.sum(jnp.where(mask2, dest, 0.0), axis=0, keepdims=True)
    dest_t_ref[...] = jnp.concatenate([d1, d2], axis=0).astype(I32)

    d1i, d2i = d1.astype(I32), d2.astype(I32)
    pieces = [(d1i >> 8).astype(F32), (d2i >> 8).astype(F32),
              (d1i & 255).astype(F32), (d2i & 255).astype(F32)]
    pieces = [p.astype(BF16) for p in pieces]
    for a, b in zip(_split3(w1), _split3(w2)):
        pieces += [a, b]
    pieces.append(jnp.zeros((16 - len(pieces), tm), BF16))
    ident = jnp.where(ti == tj, 1.0, 0.0).astype(BF16)
    cols = _dot_nt(ident, jnp.concatenate(pieces, axis=0))
    dest_ref[...] = (cols[:, 0:2] * 256.0 + cols[:, 2:4]).astype(I32)
    gate_ref[...] = cols[:, 4:6] + (cols[:, 6:8] + cols[:, 8:10])


def _router(x, w_router_t):
    n, d = x.shape
    tm = MOE_TILE
    ne = N_EXPERTS
    nt = n // tm
    return pl.pallas_call(
        _router_body,
        grid=(nt,),
        in_specs=[pl.BlockSpec((tm, d), lambda i: (i, 0)),
                  pl.BlockSpec((ne, d), lambda i: (0, 0))],
        out_specs=[pl.BlockSpec((TOP_K, tm), lambda i: (0, i)),
                   pl.BlockSpec((tm, TOP_K), lambda i: (i, 0)),
                   pl.BlockSpec((tm, TOP_K), lambda i: (i, 0)),
                   pl.BlockSpec((1, ne, V7X_LANES), lambda i: (i, 0, 0))],
        out_shape=[jax.ShapeDtypeStruct((TOP_K, n), I32),
                   jax.ShapeDtypeStruct((n, TOP_K), I32),
                   jax.ShapeDtypeStruct((n, TOP_K), F32),
                   jax.ShapeDtypeStruct((nt, ne, V7X_LANES), I32)],
        compiler_params=_cparams("parallel"),
        name="router",
    )(x, w_router_t)


def _segment_copies(tile, pc_ref, seg0_ref, off_ref, make_copy):
    out = []
    for e in range(N_EXPERTS):
        rows = pc_ref[tile * N_EXPERTS + e]
        packed0 = seg0_ref[tile * N_EXPERTS + e]
        sorted0 = off_ref[tile * N_EXPERTS + e]
        for k, size in enumerate(COPY_SIZES):
            offset = rows & ~(2 * size - 1)
            out.append(((rows & size) != 0,
                        make_copy(pl.multiple_of(packed0 + offset, MOE_ALIGN),
                                  pl.multiple_of(sorted0 + offset, MOE_ALIGN), size, e, k)))
    return out


def _start_all(copies):
    for cond, cp in copies:
        pl.when(cond)(cp.start)


def _wait_all(copies):
    for cond, cp in copies:
        pl.when(cond)(cp.wait)


def _zero_rows(zbuf, dst_hbm, row0, n_rows, sem):
    zrows = zbuf.shape[0]

    def whole(kc, carry):
        cp = pltpu.make_async_copy(
            zbuf, dst_hbm.at[pl.ds(pl.multiple_of(row0 + kc * zrows, MOE_ALIGN), zrows)], sem.at[0])
        cp.start()
        cp.wait()
        return carry
    n_whole = n_rows // zrows
    lax.fori_loop(0, n_whole, whole, 0)
    rest = n_rows - n_whole * zrows
    base = row0 + n_whole * zrows
    copies = []
    for k, size in enumerate(s for s in COPY_SIZES if s < zrows):
        offset = rest & ~(2 * size - 1)
        copies.append(((rest & size) != 0, pltpu.make_async_copy(
            zbuf.at[pl.ds(0, size)],
            dst_hbm.at[pl.ds(pl.multiple_of(base + offset, MOE_ALIGN), size)], sem.at[k])))
    _start_all(copies)
    _wait_all(copies)


def _dispatch_body(off_ref, pc_ref, seg0_ref, zoff_ref, zlen_ref, x_ref, dest_t_ref, xs_hbm,
                   buf, zbuf, sem, zsem, *, n_steps):
    t = pl.program_id(0)
    slot = t % 2
    tm = x_ref.shape[0]

    def copies(tile, sl):
        def mk(packed0, sorted0, size, e, k):
            return pltpu.make_async_copy(buf.at[sl, pl.ds(packed0, size)],
                                         xs_hbm.at[pl.ds(sorted0, size)], sem.at[sl, e, k])
        return _segment_copies(tile, pc_ref, seg0_ref, off_ref, mk)

    @pl.when(t >= 2)
    def _():
        _wait_all(copies(t - 2, slot))

    dest = dest_t_ref[...]
    j = lax.broadcasted_iota(I32, (MOE_PACK, tm), 0)
    sel = jnp.where((j == dest[0:1]) | (j == dest[1:2]), 1.0, 0.0).astype(BF16)
    buf[slot] = _dot(sel, x_ref[...].astype(BF16)).astype(BF16)
    _start_all(copies(t, slot))

    @pl.when(t == n_steps - 1)
    def _():
        if n_steps > 1:
            _wait_all(copies(t - 1, 1 - slot))
        _wait_all(copies(t, slot))
        zbuf[...] = jnp.zeros_like(zbuf)
        for e in range(N_EXPERTS):
            _zero_rows(zbuf, xs_hbm, zoff_ref[e], zlen_ref[e], zsem)
        used_rows = zoff_ref[N_EXPERTS - 1] + zlen_ref[N_EXPERTS - 1]
        _zero_rows(zbuf, xs_hbm, used_rows, xs_hbm.shape[0] - used_rows, zsem)


def _dispatch(x, dest_t, off, pc, seg0, zoff, zlen, n_rows):
    n, d = x.shape
    tm = MOE_TILE
    n_sizes = len(COPY_SIZES)
    return pl.pallas_call(
        functools.partial(_dispatch_body, n_steps=n // tm),
        grid_spec=pltpu.PrefetchScalarGridSpec(
            num_scalar_prefetch=5,
            grid=(n // tm,),
            in_specs=[pl.BlockSpec((tm, d), lambda i, *_: (i, 0)),
                      pl.BlockSpec((TOP_K, tm), lambda i, *_: (0, i))],
            out_specs=pl.BlockSpec(memory_space=pl.ANY),
            scratch_shapes=[pltpu.VMEM((2, MOE_PACK, d), BF16),
                            pltpu.VMEM((MOE_ZERO_ROWS, d), BF16),
                            pltpu.SemaphoreType.DMA((2, N_EXPERTS, n_sizes)),
                            pltpu.SemaphoreType.DMA((n_sizes,))],
        ),
        out_shape=jax.ShapeDtypeStruct((n_rows, d), BF16),
        compiler_params=_cparams("arbitrary"),
        name="moe_dispatch",
    )(off, pc, seg0, zoff, zlen, x, dest_t)


def _combine_body(off_ref, pc_ref, seg0_ref, x_ref, dest_ref, gate_ref, g_ref, b_ref, ys_hbm,
                  o_ref, ybuf, sem, *, n_steps):
    t = pl.program_id(0)
    slot = t % 2
    tm = x_ref.shape[0]

    def copies(tile, sl):
        def mk(packed0, sorted0, size, e, k):
            return pltpu.make_async_copy(ys_hbm.at[pl.ds(sorted0, size)],
                                         ybuf.at[sl, pl.ds(packed0, size)], sem.at[sl, e, k])
        return _segment_copies(tile, pc_ref, seg0_ref, off_ref, mk)

    @pl.when(t == 0)
    def _():
        ybuf[...] = jnp.zeros_like(ybuf)
        _start_all(copies(0, 0))

    @pl.when(t + 1 < n_steps)
    def _():
        _start_all(copies(t + 1, 1 - slot))

    _wait_all(copies(t, slot))
    dest = dest_ref[...]
    gate = gate_ref[...]
    y = ybuf[slot]
    lane = lax.broadcasted_iota(I32, (tm, MOE_PACK), 1)
    mix = None
    for c in range(TOP_K):
        pick = jnp.where(lane == dest[:, c:c + 1], 1.0, 0.0).astype(BF16)
        term = gate[:, c:c + 1] * _dot(pick, y)
        mix = term if mix is None else mix + term
    o_ref[...] = _layer_norm(DEEP_ALPHA * x_ref[...] + mix, g_ref[...], b_ref[...])


def _combine_ln(x, dest, gate, g, b, ys, off, pc, seg0):
    n, d = x.shape
    tm = MOE_TILE
    return pl.pallas_call(
        functools.partial(_combine_body, n_steps=n // tm),
        grid_spec=pltpu.PrefetchScalarGridSpec(
            num_scalar_prefetch=3,
            grid=(n // tm,),
            in_specs=[pl.BlockSpec((tm, d), lambda i, *_: (i, 0)),
                      pl.BlockSpec((tm, TOP_K), lambda i, *_: (i, 0)),
                      pl.BlockSpec((tm, TOP_K), lambda i, *_: (i, 0)),
                      pl.BlockSpec((1, d), lambda i, *_: (0, 0)),
                      pl.BlockSpec((1, d), lambda i, *_: (0, 0)),
                      pl.BlockSpec(memory_space=pl.ANY)],
            out_specs=pl.BlockSpec((tm, d), lambda i, *_: (i, 0)),
            scratch_shapes=[pltpu.VMEM((2, MOE_PACK, d), BF16),
                            pltpu.SemaphoreType.DMA((2, N_EXPERTS, len(COPY_SIZES)))],
        ),
        out_shape=jax.ShapeDtypeStruct((n, d), F32),
        compiler_params=_cparams("arbitrary"),
        name="moe_combine_ln",
    )(off, pc, seg0, x, dest, gate, g, b, ys)


def _moe_plan(cnt, n_tiles_max):
    tmf = ROW_TILE
    pc = (cnt + (MOE_ALIGN - 1)) // MOE_ALIGN * MOE_ALIGN
    seg0 = jnp.cumsum(pc, axis=1) - pc
    seg = jnp.sum(pc, axis=0)
    reg = (seg + (tmf - 1)) // tmf * tmf
    base = jnp.cumsum(reg) - reg
    off = base[None, :] + jnp.cumsum(pc, axis=0) - pc
    tile_end = jnp.cumsum(reg // tmf)
    nt = tile_end[-1:]
    tiles = jnp.arange(n_tiles_max, dtype=I32)
    eid = jnp.minimum(jnp.sum((tile_end[None, :] <= tiles[:, None]).astype(I32), axis=1),
                      N_EXPERTS - 1)
    flat = lambda a: a.reshape(-1).astype(I32)
    return (flat(off), flat(pc), flat(seg0), flat(base + seg), flat(reg - seg), flat(eid), flat(nt))


def _moe_ln(x, w_router, w_gu, w_down, g, b):
    n, d = x.shape
    n_tok_tiles = n // MOE_TILE
    dest_t, dest, gate, cnt = _router(x, w_router.T)
    max_rows = 2 * n + (MOE_ALIGN - 1) * N_EXPERTS * n_tok_tiles + N_EXPERTS * (ROW_TILE - MOE_ALIGN)
    n_tiles_max = -(-max_rows // ROW_TILE)
    off, pc, seg0, zoff, zlen, eid, nt = _moe_plan(cnt[:, :, 0], n_tiles_max)
    xs = _dispatch(x, dest_t, off, pc, seg0, zoff, zlen, n_tiles_max * ROW_TILE)
    ys = _ffn_grouped(xs, w_gu, w_down, eid, nt)
    return _combine_ln(x, dest, gate, g, b, ys, off, pc, seg0)


def kernel(x, dn_w_in, dn_conv_w, dn_a_log, dn_dt_bias, dn_norm_w, dn_w_out, w_kv_shared,
           sw_w_q, sw_sinks, sw_w_out, ffn_w_gu, ffn_w_down, moe_w_router, moe_w_gu,
           moe_w_down, ln_g, ln_b):
    batch, seq, d = x.shape
    n = batch * seq
    hn = DN_HEADS
    assert seq % DN_ROWS == 0 and n % ROW_TILE == 0 and n % MOE_TILE == 0
    assert dn_w_in.shape[0] == 1 and sw_w_q.shape[0] == 1
    xf = x.reshape(n, d)
    row = lambda v: v.reshape(1, -1).astype(F32)

    n_main = 4 * hn * DN_DK
    w_in = dn_w_in[0]
    w_main = w_in[:, :n_main].astype(BF16)
    w_ba = jnp.pad(w_in[:, n_main:], ((0, 0), (0, V7X_LANES - 2 * hn))).astype(BF16)
    w_bat = w_in[:, n_main:].T.astype(BF16)
    proj, ba, bat = _dn_inproj(xf, w_main, w_ba, w_bat)
    gate_params = jnp.stack([dn_a_log[0], dn_dt_bias[0]]).astype(F32)
    gp_row = jnp.pad(gate_params, ((0, 0), (hn, V7X_LANES - 2 * hn)))
    o = _deltanet_core(proj, ba, bat, dn_conv_w[0].astype(F32), gp_row, gate_params.T,
                       row(dn_norm_w[0]), batch, seq)
    x1 = _proj_ln(o, dn_w_out[0].astype(BF16), xf, row(ln_g[0, 0]), row(ln_b[0, 0]))
    x2 = _ffn_dense_ln(x1, ffn_w_gu.astype(BF16), ffn_w_down.astype(BF16),
                       row(ln_g[0, 1]), row(ln_b[0, 1]))

    w_qkv = jnp.concatenate([sw_w_q[0], w_kv_shared], axis=1).astype(BF16)
    qkv = _proj(x2, w_qkv)
    att = _swa(qkv, sw_sinks[0].astype(F32), batch, seq)
    x3 = _proj_ln(att, sw_w_out[0].astype(BF16), x2, row(ln_g[1, 0]), row(ln_b[1, 0]))
    x4 = _moe_ln(x3, moe_w_router[0].astype(F32), moe_w_gu[0].astype(BF16),
                 moe_w_down[0].astype(BF16), row(ln_g[1, 1]), row(ln_b[1, 1]))
    return x4.reshape(batch, seq, d)
```

```python
import functools

import jax
import jax.numpy as jnp
from jax import lax
from jax.experimental import pallas as pl
from jax.experimental.pallas import tpu as pltpu

F32 = jnp.float32
BF16 = jnp.bfloat16
I32 = jnp.int32

DEPTH = 2
DN_HEADS = 8
DN_DK = 128
DN_DV = 128
DN_CONV = 4
DN_CHUNK = 64
SW_Q_HEADS = 16
SW_KV_HEADS = 4
SW_HEAD_DIM = 64
SW_BLOCK = 128
N_EXPERTS = 8
TOP_K = 2
DEEP_ALPHA = (2 * DEPTH) ** 0.25
LN_EPS = 1e-5
RMS_EPS = 1e-6

V7X_VMEM_BYTES = 64 * 1024 * 1024
V7X_LANES = 128
V7X_BF16_SUBLANES = 16

VMEM_LIMIT = V7X_VMEM_BYTES - 8 * 1024 * 1024
ROW_TILE = 512
DN_ROWS = 256
MOE_TILE = 512
MOE_ALIGN = V7X_BF16_SUBLANES
MOE_PACK = TOP_K * MOE_TILE + N_EXPERTS * MOE_ALIGN
MOE_ZERO_ROWS = 256
COPY_SIZES = (512, 256, 128, 64, 32, 16)
FF_CHUNKS = ((0, 1536), (1536, 1280))


def _cparams(*sem):
    return pltpu.CompilerParams(dimension_semantics=sem, vmem_limit_bytes=VMEM_LIMIT)


def _dot(a, b):
    return jnp.dot(a, b, preferred_element_type=F32)


def _dot_nt(a, b):
    return lax.dot_general(a, b, (((1,), (1,)), ((), ())), preferred_element_type=F32)


def _dot_tn(a, b):
    return lax.dot_general(a, b, (((0,), (0,)), ((), ())), preferred_element_type=F32)


def _silu(x):
    return x * (1.0 / (1.0 + jnp.exp(-x)))


def _softplus(x):
    return jnp.maximum(x, 0.0) + jnp.log1p(jnp.exp(-jnp.abs(x)))


def _layer_norm(y, g, b):
    mu = jnp.mean(y, axis=-1, keepdims=True)
    yc = y - mu
    var = jnp.mean(yc * yc, axis=-1, keepdims=True)
    return yc * lax.rsqrt(var + LN_EPS) * g + b


def _dn_inproj_body(x_ref, w_ref, wba_ref, wbat_ref, convw_ref, proj_ref, ba_ref, bat_ref,
                    stage_ref, *, tiles_per_seq):
    hn, dk = DN_HEADS, DN_DK
    rows = x_ref.shape[0]
    width = hn * dk

    @pl.when(pl.program_id(0) % tiles_per_seq == 0)
    def _():
        stage_ref[0:8, :] = jnp.zeros((8, stage_ref.shape[1]), F32)

    xb = x_ref[...].astype(BF16)
    ba_ref[...] = _dot(xb, wba_ref[...])
    bat_ref[...] = _dot_nt(wbat_ref[...], xb)
    for part in range(4):
        raw = _dot(xb, w_ref[:, part * width:(part + 1) * width])
        if part < 3:
            stage_ref[8:8 + rows, part * width:(part + 1) * width] = raw
        for h in range(hn):
            col0 = part * width + h * dk
            if part < 3:
                w = convw_ref[:, col0:col0 + dk]
                a = None
                for j in range(DN_CONV):
                    term = stage_ref[8 - j:8 - j + rows, col0:col0 + dk] * w[DN_CONV - 1 - j:DN_CONV - j]
                    a = term if a is None else a + term
            else:
                a = raw[:, h * dk:(h + 1) * dk]
            a = _silu(a)
            if part < 2:
                scale = lax.rsqrt(jnp.sum(a * a, axis=-1, keepdims=True) + RMS_EPS)
                a = a * (scale * (dk ** -0.5) if part == 0 else scale)
            proj_ref[:, col0:col0 + dk] = a
    stage_ref[0:8, :] = stage_ref[rows:rows + 8, :]


def _dn_inproj(x, w_main, w_ba, w_bat, conv_w, seq):
    n, d = x.shape
    n_out = w_main.shape[1]
    n_qkv = conv_w.shape[1]
    tm = ROW_TILE
    assert seq % tm == 0
    return pl.pallas_call(
        functools.partial(_dn_inproj_body, tiles_per_seq=seq // tm),
        grid=(n // tm,),
        in_specs=[
            pl.BlockSpec((tm, d), lambda i: (i, 0)),
            pl.BlockSpec((d, n_out), lambda i: (0, 0)),
            pl.BlockSpec((d, V7X_LANES), lambda i: (0, 0)),
            pl.BlockSpec((2 * DN_HEADS, d), lambda i: (0, 0)),
            pl.BlockSpec((DN_CONV, n_qkv), lambda i: (0, 0)),
        ],
        out_specs=[
            pl.BlockSpec((tm, n_out), lambda i: (i, 0)),
            pl.BlockSpec((tm, V7X_LANES), lambda i: (i, 0)),
            pl.BlockSpec((2 * DN_HEADS, tm), lambda i: (0, i)),
        ],
        out_shape=[
            jax.ShapeDtypeStruct((n, n_out), F32),
            jax.ShapeDtypeStruct((n, V7X_LANES), F32),
            jax.ShapeDtypeStruct((2 * DN_HEADS, n), F32),
        ],
        scratch_shapes=[pltpu.VMEM((8 + tm, n_qkv), F32)],
        compiler_params=_cparams("arbitrary"),
        name="dn_inproj",
    )(x, w_main, w_ba, w_bat, conv_w)


def _proj_body(x_ref, w_ref, o_ref):
    o_ref[...] = _dot(x_ref[...].astype(BF16), w_ref[...])


def _proj(x, w):
    n, k = x.shape
    m = w.shape[1]
    tm = ROW_TILE
    return pl.pallas_call(
        _proj_body,
        grid=(n // tm,),
        in_specs=[pl.BlockSpec((tm, k), lambda i: (i, 0)),
                  pl.BlockSpec((k, m), lambda i: (0, 0))],
        out_specs=pl.BlockSpec((tm, m), lambda i: (i, 0)),
        out_shape=jax.ShapeDtypeStruct((n, m), F32),
        compiler_params=_cparams("parallel"),
        name="proj",
    )(x, w)


def _proj_ln_body(a_ref, w_ref, res_ref, g_ref, b_ref, o_ref):
    h = _dot(a_ref[...].astype(BF16), w_ref[...])
    o_ref[...] = _layer_norm(DEEP_ALPHA * res_ref[...] + h, g_ref[...], b_ref[...])


def _proj_ln(a, w, res, g, b):
    n, k = a.shape
    d = w.shape[1]
    tm = ROW_TILE
    return pl.pallas_call(
        _proj_ln_body,
        grid=(n // tm,),
        in_specs=[pl.BlockSpec((tm, k), lambda i: (i, 0)),
                  pl.BlockSpec((k, d), lambda i: (0, 0)),
                  pl.BlockSpec((tm, d), lambda i: (i, 0)),
                  pl.BlockSpec((1, d), lambda i: (0, 0)),
                  pl.BlockSpec((1, d), lambda i: (0, 0))],
        out_specs=pl.BlockSpec((tm, d), lambda i: (i, 0)),
        out_shape=jax.ShapeDtypeStruct((n, d), F32),
        compiler_params=_cparams("parallel"),
        name="proj_ln",
    )(a, w, res, g, b)


def _dn_body(proj_ref, ba_ref, bat_ref, gprow_ref, gpcol_ref, normw_ref, o_ref, s_ref):
    rows = proj_ref.shape[0]
    hn, dk, ck = DN_HEADS, DN_DK, DN_CHUNK
    n_chunks = rows // ck
    n_qkv = 3 * hn * dk

    @pl.when(pl.program_id(1) == 0)
    def _():
        s_ref[...] = jnp.zeros_like(s_ref)

    ba = ba_ref[...]
    beta_all = 1.0 / (1.0 + jnp.exp(-ba))
    gprow = gprow_ref[...]
    g_all = -jnp.exp(gprow[0:1]) * _softplus(ba + gprow[1:2])
    row_in_chunk = lax.broadcasted_iota(I32, g_all.shape, 0) & (ck - 1)
    gcum = g_all
    step = 1
    while step < ck:
        gcum = gcum + jnp.where(row_in_chunk >= step, pltpu.roll(gcum, step, 0), 0.0)
        step *= 2
    gpcol = gpcol_ref[...]
    gt = -jnp.exp(gpcol[:, 0:1]) * _softplus(bat_ref[hn:2 * hn, :] + gpcol[:, 1:2])
    lane_in_chunk = lax.broadcasted_iota(I32, gt.shape, 1) & (ck - 1)
    gtcum = gt
    step = 1
    while step < ck:
        gtcum = gtcum + jnp.where(lane_in_chunk >= step, pltpu.roll(gtcum, step, 1), 0.0)
        step *= 2

    ri = lax.broadcasted_iota(I32, (n_chunks, ck, ck), 1)
    ci = lax.broadcasted_iota(I32, (n_chunks, ck, ck), 2)
    strict = ri > ci
    eye = ri == ci

    def chunked(a):
        return a.reshape(n_chunks, ck, a.shape[-1])

    def bmm(a, b):
        return jnp.einsum('cij,cjk->cik', a, b, preferred_element_type=F32)

    def bmm_nt(a, b):
        return jnp.einsum('cid,cjd->cij', a, b, preferred_element_type=F32)

    heads = range(hn)
    gc = [gcum[:, hn + h:hn + h + 1] for h in heads]
    gr = [jnp.stack([gtcum[h:h + 1, c * ck:(c + 1) * ck] for c in range(n_chunks)])
          for h in heads]
    dec = [jnp.exp(jnp.where(strict, chunked(gc[h]) - gr[h], -jnp.inf)) for h in heads]
    eg = [jnp.exp(gc[h]) for h in heads]
    q = [proj_ref[:, h * dk:(h + 1) * dk] for h in heads]
    k = [proj_ref[:, (hn + h) * dk:(hn + h + 1) * dk] for h in heads]
    kb = [k[h] * beta_all[:, h:h + 1] for h in heads]
    vb = [proj_ref[:, (2 * hn + h) * dk:(2 * hn + h + 1) * dk] * beta_all[:, h:h + 1]
          for h in heads]

    kc = [chunked(k[h]).astype(BF16) for h in heads]
    gram = [bmm_nt(jnp.concatenate([chunked(kb[h]), chunked(q[h])], axis=1).astype(BF16), kc[h])
            for h in heads]
    aqk = [(gram[h][:, ck:] * jnp.where(eye, 1.0, dec[h])).astype(BF16) for h in heads]
    left = lax.broadcasted_iota(I32, (n_chunks, ck, 2 * ck), 2) < ck
    x = [jnp.concatenate([-(gram[h][:, 0:ck] * dec[h]), jnp.where(eye, 1.0, 0.0)], axis=2)
         for h in heads]
    for _ in range(6):
        xb = [x[h].astype(BF16) for h in heads]
        bx = [bmm(xb[h][:, :, 0:ck], xb[h]) for h in heads]
        x = [jnp.where(left, bx[h], x[h] + bx[h]) for h in heads]
    uw = [bmm(x[h][:, :, ck:].astype(BF16),
              chunked(jnp.concatenate([vb[h], kb[h] * eg[h]], axis=1)).astype(BF16))
          for h in heads]
    wq = [jnp.concatenate([uw[h][:, :, DN_DV:], chunked(q[h] * eg[h])], axis=2).astype(BF16)
          for h in heads]
    kdec = [[(k[h][c * ck:(c + 1) * ck]
              * jnp.exp(gc[h][(c + 1) * ck - 1:(c + 1) * ck] - gc[h][c * ck:(c + 1) * ck])
              ).astype(BF16) for c in range(n_chunks)] for h in heads]

    s = [s_ref[h] for h in heads]
    zero_s = jnp.zeros((dk, DN_DV), BF16)
    vnew = [[] for _ in heads]
    qs = [[] for _ in heads]
    for c in range(n_chunks):
        r0 = c * ck
        sb = [s[h].astype(BF16) for h in heads]
        s2 = [jnp.concatenate([jnp.concatenate([sb[h], zero_s], axis=1),
                               jnp.concatenate([zero_s, sb[h]], axis=1)], axis=0) for h in heads]
        ws_qs = [_dot(wq[h][c], s2[h]) for h in heads]
        for h in heads:
            vn = uw[h][c][:, 0:DN_DV] - ws_qs[h][:, 0:DN_DV]
            vnew[h].append(vn.astype(BF16))
            qs[h].append(ws_qs[h][:, DN_DV:])
        upd = [_dot_tn(kdec[h][c], vnew[h][c]) for h in heads]
        s = [s[h] * jnp.exp(gc[h][r0 + ck - 1:r0 + ck]) + upd[h] for h in heads]
    for h in heads:
        s_ref[h] = s[h]
    o_all = [(jnp.stack(qs[h]) + bmm(aqk[h], jnp.stack(vnew[h]))).reshape(rows, DN_DV)
             for h in heads]
    for h in heads:
        o = o_all[h]
        o = o * lax.rsqrt(jnp.mean(o * o, axis=-1, keepdims=True) + RMS_EPS) * normw_ref[...]
        zs = proj_ref[:, n_qkv + h * DN_DV:n_qkv + (h + 1) * DN_DV]
        o_ref[:, h * DN_DV:(h + 1) * DN_DV] = (o * zs).astype(o_ref.dtype)


def _deltanet_core(proj, ba, bat, gp_row, gp_col, norm_w, batch, seq):
    n = proj.shape[0]
    r = DN_ROWS
    nt = seq // r
    return pl.pallas_call(
        _dn_body,
        grid=(batch, nt),
        in_specs=[
            pl.BlockSpec((r, proj.shape[1]), lambda b, t: (b * nt + t, 0)),
            pl.BlockSpec((r, V7X_LANES), lambda b, t: (b * nt + t, 0)),
            pl.BlockSpec((2 * DN_HEADS, r), lambda b, t: (0, b * nt + t)),
            pl.BlockSpec((2, V7X_LANES), lambda b, t: (0, 0)),
            pl.BlockSpec((DN_HEADS, 2), lambda b, t: (0, 0)),
            pl.BlockSpec((1, DN_DV), lambda b, t: (0, 0)),
        ],
        out_specs=pl.BlockSpec((r, DN_HEADS * DN_DV), lambda b, t: (b * nt + t, 0)),
        out_shape=jax.ShapeDtypeStruct((n, DN_HEADS * DN_DV), BF16),
        scratch_shapes=[pltpu.VMEM((DN_HEADS, DN_DK, DN_DV), F32)],
        compiler_params=_cparams("arbitrary", "arbitrary"),
        name="deltanet",
    )(proj, ba, bat, gp_row, gp_col, norm_w)


def _swa_body(sink_ref, q_ref, kc_ref, kp_ref, vc_ref, vp_ref, o_ref, bias_ref):
    w, hd = SW_BLOCK, SW_HEAD_DIM
    group = SW_Q_HEADS // SW_KV_HEADS
    heads = range(SW_Q_HEADS)

    @pl.when((pl.program_id(0) == 0) & (pl.program_id(1) == 0))
    def _():
        qi = lax.broadcasted_iota(I32, (w, 2 * w), 0)
        sj = lax.broadcasted_iota(I32, (w, 2 * w), 1)
        dist = qi + w - sj
        valid = (dist >= 0) & (dist < w)
        distf = dist.astype(F32)
        for hq in heads:
            slope = 2.0 ** (-8.0 * (hq + 1) / SW_Q_HEADS)
            bias = jnp.where(valid, -(slope * distf), -jnp.inf)
            bias_ref[1, hq] = bias
            bias_ref[0, hq] = jnp.where(sj >= w, bias, -jnp.inf)

    has_prev = jnp.where(pl.program_id(1) > 0, 1, 0)
    qb = (q_ref[...] * (hd ** -0.5)).astype(BF16)
    kband, vband = [], []
    for kh in range(SW_KV_HEADS):
        cs = slice(kh * hd, (kh + 1) * hd)
        kband.append(jnp.concatenate([kp_ref[:, cs], kc_ref[:, cs]], axis=0).astype(BF16))
        vband.append(jnp.concatenate([vp_ref[:, cs], vc_ref[:, cs]], axis=0).astype(BF16))
    s = [_dot_nt(qb[:, hq * hd:(hq + 1) * hd], kband[hq // group]) + bias_ref[has_prev, hq]
         for hq in heads]
    m = [jnp.maximum(jnp.max(s[hq], axis=-1, keepdims=True), sink_ref[hq]) for hq in heads]
    p = [jnp.exp(s[hq] - m[hq]) for hq in heads]
    den = [jnp.sum(p[hq], axis=-1, keepdims=True) + jnp.exp(sink_ref[hq] - m[hq]) for hq in heads]
    o = [_dot(p[hq].astype(BF16), vband[hq // group]) / den[hq] for hq in heads]
    for hq in heads:
        o_ref[:, hq * hd:(hq + 1) * hd] = o[hq].astype(o_ref.dtype)


def _swa(qkv, sinks, batch, seq):
    n = qkv.shape[0]
    w = SW_BLOCK
    nb = seq // w
    nq = SW_Q_HEADS * SW_HEAD_DIM
    nkv = SW_KV_HEADS * SW_HEAD_DIM
    kcol = nq // nkv
    cur = lambda b, i: b * nb + i
    prev = lambda b, i: b * nb + jnp.maximum(i - 1, 0)
    return pl.pallas_call(
        _swa_body,
        grid=(batch, nb),
        in_specs=[
            pl.BlockSpec(memory_space=pltpu.SMEM),
            pl.BlockSpec((w, nq), lambda b, i: (cur(b, i), 0)),
            pl.BlockSpec((w, nkv), lambda b, i: (cur(b, i), kcol)),
            pl.BlockSpec((w, nkv), lambda b, i: (prev(b, i), kcol)),
            pl.BlockSpec((w, nkv), lambda b, i: (cur(b, i), kcol + 1)),
            pl.BlockSpec((w, nkv), lambda b, i: (prev(b, i), kcol + 1)),
        ],
        out_specs=pl.BlockSpec((w, nq), lambda b, i: (cur(b, i), 0)),
        out_shape=jax.ShapeDtypeStruct((n, nq), BF16),
        scratch_shapes=[pltpu.VMEM((2, SW_Q_HEADS, w, 2 * w), F32)],
        compiler_params=_cparams("arbitrary", "arbitrary"),
        name="swa",
    )(sinks, qkv, qkv, qkv, qkv, qkv)


def _ffn_tile(x_ref, wgu_ref, wd_ref):
    xb = x_ref[...].astype(BF16)
    d_ff = wd_ref.shape[1]
    y = None
    for c0, cw in FF_CHUNKS:
        hg = _dot(xb, wgu_ref[0, :, c0:c0 + cw])
        hu = _dot(xb, wgu_ref[0, :, d_ff + c0:d_ff + c0 + cw])
        act = (_silu(hg) * hu).astype(BF16)
        part = _dot(act, wd_ref[0, c0:c0 + cw, :])
        y = part if y is None else y + part
    return y


def _ffn_group_body(eid_ref, nt_ref, x_ref, wgu_ref, wd_ref, o_ref):
    used = pl.program_id(0) < nt_ref[0]

    @pl.when(used)
    def _():
        o_ref[...] = _ffn_tile(x_ref, wgu_ref, wd_ref).astype(o_ref.dtype)

    @pl.when(jnp.logical_not(used))
    def _():
        o_ref[...] = jnp.zeros_like(o_ref)


def _ffn_grouped(xs, w_gu, w_down, eid, nt):
    r, d = xs.shape
    tm = ROW_TILE
    d_ff = w_down.shape[1]
    assert sum(cw for _, cw in FF_CHUNKS) == d_ff
    tile = lambda i, eid_ref, nt_ref: jnp.minimum(i, nt_ref[0] - 1)
    return pl.pallas_call(
        _ffn_group_body,
        grid_spec=pltpu.PrefetchScalarGridSpec(
            num_scalar_prefetch=2,
            grid=(r // tm,),
            in_specs=[
                pl.BlockSpec((tm, d), lambda i, e, n: (tile(i, e, n), 0)),
                pl.BlockSpec((1, d, 2 * d_ff), lambda i, e, n: (e[tile(i, e, n)], 0, 0)),
                pl.BlockSpec((1, d_ff, d), lambda i, e, n: (e[tile(i, e, n)], 0, 0)),
            ],
            out_specs=pl.BlockSpec((tm, d), lambda i, e, n: (i, 0)),
        ),
        out_shape=jax.ShapeDtypeStruct((r, d), BF16),
        compiler_params=_cparams("arbitrary"),
        name="ffn_grouped",
    )(eid, nt, xs, w_gu, w_down)


def _ffn_ln_body(x_ref, wgu_ref, wd_ref, g_ref, b_ref, o_ref):
    y = _ffn_tile(x_ref, wgu_ref, wd_ref)
    o_ref[...] = _layer_norm(DEEP_ALPHA * x_ref[...] + y, g_ref[...], b_ref[...])


def _ffn_dense_ln(x, w_gu, w_down, g, b):
    n, d = x.shape
    tm = ROW_TILE
    d_ff = w_down.shape[1]
    assert sum(cw for _, cw in FF_CHUNKS) == d_ff
    return pl.pallas_call(
        _ffn_ln_body,
        grid=(n // tm,),
        in_specs=[pl.BlockSpec((tm, d), lambda i: (i, 0)),
                  pl.BlockSpec((1, d, 2 * d_ff), lambda i: (0, 0, 0)),
                  pl.BlockSpec((1, d_ff, d), lambda i: (0, 0, 0)),
                  pl.BlockSpec((1, d), lambda i: (0, 0)),
                  pl.BlockSpec((1, d), lambda i: (0, 0))],
        out_specs=pl.BlockSpec((tm, d), lambda i: (i, 0)),
        out_shape=jax.ShapeDtypeStruct((n, d), F32),
        compiler_params=_cparams("parallel"),
        name="ffn_dense_ln",
    )(x, w_gu, w_down, g, b)


def _split3(a):
    hi = a.astype(BF16)
    r1 = a - hi.astype(F32)
    mid = r1.astype(BF16)
    lo = (r1 - mid.astype(F32)).astype(BF16)
    return hi, mid, lo


def _router_body(x_ref, w_ref, dest_t_ref, dest_ref, gate_ref, cnt_ref):
    tm = x_ref.shape[0]
    ne = N_EXPERTS
    xh, xm, xl = _split3(x_ref[...])
    wst = jnp.concatenate(_split3(w_ref[...]), axis=0)
    th = _dot_nt(wst, xh)
    tmid = _dot_nt(wst[0:2 * ne], xm)
    tl = _dot_nt(wst[0:ne], xl)
    logits = ((th[2 * ne:] + tmid[ne:] + tl) + (th[ne:2 * ne] + tmid[0:ne])) + th[0:ne]

    eidx = lax.broadcasted_iota(I32, (ne, tm), 0)
    m1 = jnp.max(logits, axis=0, keepdims=True)
    i1 = jnp.min(jnp.where(logits == m1, eidx, ne), axis=0, keepdims=True)
    mask1 = eidx == i1
    rest = jnp.where(mask1, -jnp.inf, logits)
    m2 = jnp.max(rest, axis=0, keepdims=True)
    i2 = jnp.min(jnp.where(rest == m2, eidx, ne), axis=0, keepdims=True)
    mask2 = eidx == i2
    ex = jnp.exp(m2 - m1)
    w1 = 1.0 / (1.0 + ex)
    w2 = ex / (1.0 + ex)
    sel = mask1 | mask2

    ti = lax.broadcasted_iota(I32, (tm, tm), 0)
    tj = lax.broadcasted_iota(I32, (tm, tm), 1)
    before = jnp.where(ti < tj, 1.0, 0.0).astype(BF16)
    self_f = jnp.where(sel, 1.0, 0.0)
    rank = _dot(self_f.astype(BF16), before)
    cnt = jnp.sum(self_f, axis=1, keepdims=True).astype(I32)
    cnt_ref[0] = jnp.broadcast_to(cnt, (ne, V7X_LANES))

    seg_rows = (((cnt + (MOE_ALIGN - 1)) // MOE_ALIGN) * MOE_ALIGN).astype(F32)
    seg_rows = jnp.broadcast_to(seg_rows, (ne, V7X_LANES))
    e8 = lax.broadcasted_iota(I32, (ne, V7X_LANES), 0)
    seg0 = jnp.zeros((ne, V7X_LANES), F32)
    for k in range(1, ne):
        seg0 = seg0 + jnp.where(e8 >= k, pltpu.roll(seg_rows, k, 0), 0.0)
    dest = seg0[:, 0:1] + rank
    d1 = jnp.sum(jnp.where(mask1, dest, 0.0), axis=0, keepdims=True)
    d2 = jnp.sum(jnp.where(mask2, dest, 0.0), axis=0, keepdims=True)
    dest_t_ref[...] = jnp.concatenate([d1, d2], axis=0).astype(I32)

    d1i, d2i = d1.astype(I32), d2.astype(I32)
    pieces = [(d1i >> 8).astype(F32), (d2i >> 8).astype(F32),
              (d1i & 255).astype(F32), (d2i & 255).astype(F32)]
    pieces = [p.astype(BF16) for p in pieces]
    for a, b in zip(_split3(w1), _split3(w2)):
        pieces += [a, b]
    pieces.append(jnp.zeros((16 - len(pieces), tm), BF16))
    ident = jnp.where(ti == tj, 1.0, 0.0).astype(BF16)
    cols = _dot_nt(ident, jnp.concatenate(pieces, axis=0))
    dest_ref[...] = (cols[:, 0:2] * 256.0 + cols[:, 2:4]).astype(I32)
    gate_ref[...] = cols[:, 4:6] + (cols[:, 6:8] + cols[:, 8:10])


def _router(x, w_router_t):
    n, d = x.shape
    tm = MOE_TILE
    ne = N_EXPERTS
    nt = n // tm
    return pl.pallas_call(
        _router_body,
        grid=(nt,),
        in_specs=[pl.BlockSpec((tm, d), lambda i: (i, 0)),
                  pl.BlockSpec((ne, d), lambda i: (0, 0))],
        out_specs=[pl.BlockSpec((TOP_K, tm), lambda i: (0, i)),
                   pl.BlockSpec((tm, TOP_K), lambda i: (i, 0)),
                   pl.BlockSpec((tm, TOP_K), lambda i: (i, 0)),
                   pl.BlockSpec((1, ne, V7X_LANES), lambda i: (i, 0, 0))],
        out_shape=[jax.ShapeDtypeStruct((TOP_K, n), I32),
                   jax.ShapeDtypeStruct((n, TOP_K), I32),
                   jax.ShapeDtypeStruct((n, TOP_K), F32),
                   jax.ShapeDtypeStruct((nt, ne, V7X_LANES), I32)],
        compiler_params=_cparams("parallel"),
        name="router",
    )(x, w_router_t)


def _segment_copies(tile, pc_ref, seg0_ref, off_ref, make_copy):
    out = []
    for e in range(N_EXPERTS):
        rows = pc_ref[tile * N_EXPERTS + e]
        packed0 = seg0_ref[tile * N_EXPERTS + e]
        sorted0 = off_ref[tile * N_EXPERTS + e]
        for k, size in enumerate(COPY_SIZES):
            offset = rows & ~(2 * size - 1)
            out.append(((rows & size) != 0,
                        make_copy(pl.multiple_of(packed0 + offset, MOE_ALIGN),
                                  pl.multiple_of(sorted0 + offset, MOE_ALIGN), size, e, k)))
    return out


def _start_all(copies):
    for cond, cp in copies:
        pl.when(cond)(cp.start)


def _wait_all(copies):
    for cond, cp in copies:
        pl.when(cond)(cp.wait)


def _zero_rows(zbuf, dst_hbm, row0, n_rows, sem):
    zrows = zbuf.shape[0]

    def whole(kc, carry):
        cp = pltpu.make_async_copy(
            zbuf, dst_hbm.at[pl.ds(pl.multiple_of(row0 + kc * zrows, MOE_ALIGN), zrows)], sem.at[0])
        cp.start()
        cp.wait()
        return carry
    n_whole = n_rows // zrows
    lax.fori_loop(0, n_whole, whole, 0)
    rest = n_rows - n_whole * zrows
    base = row0 + n_whole * zrows
    copies = []
    for k, size in enumerate(s for s in COPY_SIZES if s < zrows):
        offset = rest & ~(2 * size - 1)
        copies.append(((rest & size) != 0, pltpu.make_async_copy(
            zbuf.at[pl.ds(0, size)],
            dst_hbm.at[pl.ds(pl.multiple_of(base + offset, MOE_ALIGN), size)], sem.at[k])))
    _start_all(copies)
    _wait_all(copies)


def _dispatch_body(off_ref, pc_ref, seg0_ref, zoff_ref, zlen_ref, x_ref, dest_t_ref, xs_hbm,
                   buf, zbuf, sem, zsem, *, n_steps):
    t = pl.program_id(0)
    slot = t % 2
    tm = x_ref.shape[0]

    def copies(tile, sl):
        def mk(packed0, sorted0, size, e, k):
            return pltpu.make_async_copy(buf.at[sl, pl.ds(packed0, size)],
                                         xs_hbm.at[pl.ds(sorted0, size)], sem.at[sl, e, k])
        return _segment_copies(tile, pc_ref, seg0_ref, off_ref, mk)

    @pl.when(t >= 2)
    def _():
        _wait_all(copies(t - 2, slot))

    dest = dest_t_ref[...]
    j = lax.broadcasted_iota(I32, (MOE_PACK, tm), 0)
    sel = jnp.where((j == dest[0:1]) | (j == dest[1:2]), 1.0, 0.0).astype(BF16)
    buf[slot] = _dot(sel, x_ref[...].astype(BF16)).astype(BF16)
    _start_all(copies(t, slot))

    @pl.when(t == n_steps - 1)
    def _():
        if n_steps > 1:
            _wait_all(copies(t - 1, 1 - slot))
        _wait_all(copies(t, slot))
        zbuf[...] = jnp.zeros_like(zbuf)
        for e in range(N_EXPERTS):
            _zero_rows(zbuf, xs_hbm, zoff_ref[e], zlen_ref[e], zsem)
        used_rows = zoff_ref[N_EXPERTS - 1] + zlen_ref[N_EXPERTS - 1]
        _zero_rows(zbuf, xs_hbm, used_rows, xs_hbm.shape[0] - used_rows, zsem)


def _dispatch(x, dest_t, off, pc, seg0, zoff, zlen, n_rows):
    n, d = x.shape
    tm = MOE_TILE
    n_sizes = len(COPY_SIZES)
    return pl.pallas_call(
        functools.partial(_dispatch_body, n_steps=n // tm),
        grid_spec=pltpu.PrefetchScalarGridSpec(
            num_scalar_prefetch=5,
            grid=(n // tm,),
            in_specs=[pl.BlockSpec((tm, d), lambda i, *_: (i, 0)),
                      pl.BlockSpec((TOP_K, tm), lambda i, *_: (0, i))],
            out_specs=pl.BlockSpec(memory_space=pl.ANY),
            scratch_shapes=[pltpu.VMEM((2, MOE_PACK, d), BF16),
                            pltpu.VMEM((MOE_ZERO_ROWS, d), BF16),
                            pltpu.SemaphoreType.DMA((2, N_EXPERTS, n_sizes)),
                            pltpu.SemaphoreType.DMA((n_sizes,))],
        ),
        out_shape=jax.ShapeDtypeStruct((n_rows, d), BF16),
        compiler_params=_cparams("arbitrary"),
        name="moe_dispatch",
    )(off, pc, seg0, zoff, zlen, x, dest_t)


def _combine_body(off_ref, pc_ref, seg0_ref, x_ref, dest_ref, gate_ref, g_ref, b_ref, ys_hbm,
                  o_ref, ybuf, sem, *, n_steps):
    t = pl.program_id(0)
    slot = t % 2
    tm = x_ref.shape[0]

    def copies(tile, sl):
        def mk(packed0, sorted0, size, e, k):
            return pltpu.make_async_copy(ys_hbm.at[pl.ds(sorted0, size)],
                                         ybuf.at[sl, pl.ds(packed0, size)], sem.at[sl, e, k])
        return _segment_copies(tile, pc_ref, seg0_ref, off_ref, mk)

    @pl.when(t == 0)
    def _():
        ybuf[...] = jnp.zeros_like(ybuf)
        _start_all(copies(0, 0))

    @pl.when(t + 1 < n_steps)
    def _():
        _start_all(copies(t + 1, 1 - slot))

    _wait_all(copies(t, slot))
    dest = dest_ref[...]
    gate = gate_ref[...]
    y = ybuf[slot]
    lane = lax.broadcasted_iota(I32, (tm, MOE_PACK), 1)
    mix = None
    for c in range(TOP_K):
        pick = jnp.where(lane == dest[:, c:c + 1], 1.0, 0.0).astype(BF16)
        term = gate[:, c:c + 1] * _dot(pick, y)
        mix = term if mix is None else mix + term
    o_ref[...] = _layer_norm(DEEP_ALPHA * x_ref[...] + mix, g_ref[...], b_ref[...])


def _combine_ln(x, dest, gate, g, b, ys, off, pc, seg0):
    n, d = x.shape
    tm = MOE_TILE
    return pl.pallas_call(
        functools.partial(_combine_body, n_steps=n // tm),
        grid_spec=pltpu.PrefetchScalarGridSpec(
            num_scalar_prefetch=3,
            grid=(n // tm,),
            in_specs=[pl.BlockSpec((tm, d), lambda i, *_: (i, 0)),
                      pl.BlockSpec((tm, TOP_K), lambda i, *_: (i, 0)),
                      pl.BlockSpec((tm, TOP_K), lambda i, *_: (i, 0)),
                      pl.BlockSpec((1, d), lambda i, *_: (0, 0)),
                      pl.BlockSpec((1, d), lambda i, *_: (0, 0)),
                      pl.BlockSpec(memory_space=pl.ANY)],
            out_specs=pl.BlockSpec((tm, d), lambda i, *_: (i, 0)),
            scratch_shapes=[pltpu.VMEM((2, MOE_PACK, d), BF16),
                            pltpu.SemaphoreType.DMA((2, N_EXPERTS, len(COPY_SIZES)))],
        ),
        out_shape=jax.ShapeDtypeStruct((n, d), F32),
        compiler_params=_cparams("arbitrary"),
        name="moe_combine_ln",
    )(off, pc, seg0, x, dest, gate, g, b, ys)


def _moe_plan(cnt, n_tiles_max):
    tmf = ROW_TILE
    pc = (cnt + (MOE_ALIGN - 1)) // MOE_ALIGN * MOE_ALIGN
    seg0 = jnp.cumsum(pc, axis=1) - pc
    seg = jnp.sum(pc, axis=0)
    reg = (seg + (tmf - 1)) // tmf * tmf
    base = jnp.cumsum(reg) - reg
    off = base[None, :] + jnp.cumsum(pc, axis=0) - pc
    tile_end = jnp.cumsum(reg // tmf)
    nt = tile_end[-1:]
    tiles = jnp.arange(n_tiles_max, dtype=I32)
    eid = jnp.minimum(jnp.sum((tile_end[None, :] <= tiles[:, None]).astype(I32), axis=1),
                      N_EXPERTS - 1)
    flat = lambda a: a.reshape(-1).astype(I32)
    return (flat(off), flat(pc), flat(seg0), flat(base + seg), flat(reg - seg), flat(eid), flat(nt))


def _moe_ln(x, w_router, w_gu, w_down, g, b):
    n, d = x.shape
    n_tok_tiles = n // MOE_TILE
    dest_t, dest, gate, cnt = _router(x, w_router.T)
    max_rows = 2 * n + (MOE_ALIGN - 1) * N_EXPERTS * n_tok_tiles + N_EXPERTS * (ROW_TILE - MOE_ALIGN)
    n_tiles_max = -(-max_rows // ROW_TILE)
    off, pc, seg0, zoff, zlen, eid, nt = _moe_plan(cnt[:, :, 0], n_tiles_max)
    xs = _dispatch(x, dest_t, off, pc, seg0, zoff, zlen, n_tiles_max * ROW_TILE)
    ys = _ffn_grouped(xs, w_gu, w_down, eid, nt)
    return _combine_ln(x, dest, gate, g, b, ys, off, pc, seg0)


def kernel(x, dn_w_in, dn_conv_w, dn_a_log, dn_dt_bias, dn_norm_w, dn_w_out, w_kv_shared,
           sw_w_q, sw_sinks, sw_w_out, ffn_w_gu, ffn_w_down, moe_w_router, moe_w_gu,
           moe_w_down, ln_g, ln_b):
    batch, seq, d = x.shape
    n = batch * seq
    hn = DN_HEADS
    assert seq % DN_ROWS == 0 and n % ROW_TILE == 0 and n % MOE_TILE == 0
    assert dn_w_in.shape[0] == 1 and sw_w_q.shape[0] == 1
    xf = x.reshape(n, d)
    row = lambda v: v.reshape(1, -1).astype(F32)

    n_main = 4 * hn * DN_DK
    w_in = dn_w_in[0]
    w_main = w_in[:, :n_main].astype(BF16)
    w_ba = jnp.pad(w_in[:, n_main:], ((0, 0), (0, V7X_LANES - 2 * hn))).astype(BF16)
    w_bat = w_in[:, n_main:].T.astype(BF16)
    proj, ba, bat = _dn_inproj(xf, w_main, w_ba, w_bat, dn_conv_w[0].astype(F32), seq)
    gate_params = jnp.stack([dn_a_log[0], dn_dt_bias[0]]).astype(F32)
    gp_row = jnp.pad(gate_params, ((0, 0), (hn, V7X_LANES - 2 * hn)))
    o = _deltanet_core(proj, ba, bat, gp_row, gate_params.T, row(dn_norm_w[0]), batch, seq)
    x1 = _proj_ln(o, dn_w_out[0].astype(BF16), xf, row(ln_g[0, 0]), row(ln_b[0, 0]))
    x2 = _ffn_dense_ln(x1, ffn_w_gu.astype(BF16), ffn_w_down.astype(BF16),
                       row(ln_g[0, 1]), row(ln_b[0, 1]))

    w_qkv = jnp.concatenate([sw_w_q[0], w_kv_shared], axis=1).astype(BF16)
    qkv = _proj(x2, w_qkv)
    att = _swa(qkv, sw_sinks[0].astype(F32), batch, seq)
    x3 = _proj_ln(att, sw_w_out[0].astype(BF16), x2, row(ln_g[1, 0]), row(ln_b[1, 0]))
    x4 = _moe_ln(x3, moe_w_router[0].astype(F32), moe_w_gu[0].astype(BF16),
                 moe_w_down[0].astype(BF16), row(ln_g[1, 1]), row(ln_b[1, 1]))
    return x4.reshape(batch, seq, d)
```

```python
import functools

import jax
import jax.numpy as jnp
from jax import lax
from jax.experimental import pallas as pl
from jax.experimental.pallas import tpu as pltpu

F32 = jnp.float32
BF16 = jnp.bfloat16
I32 = jnp.int32

DEPTH = 2
DN_HEADS = 8
DN_DK = 128
DN_DV = 128
DN_CONV = 4
DN_CHUNK = 64
SW_Q_HEADS = 16
SW_KV_HEADS = 4
SW_HEAD_DIM = 64
SW_BLOCK = 128
N_EXPERTS = 8
TOP_K = 2
DEEP_ALPHA = (2 * DEPTH) ** 0.25
LN_EPS = 1e-5
RMS_EPS = 1e-6

V7X_VMEM_BYTES = 64 * 1024 * 1024
V7X_LANES = 128
V7X_BF16_SUBLANES = 16

VMEM_LIMIT = V7X_VMEM_BYTES - 8 * 1024 * 1024
ROW_TILE = 512
DN_ROWS = 256
MOE_TILE = 512
MOE_ALIGN = V7X_BF16_SUBLANES
MOE_PACK = TOP_K * MOE_TILE + N_EXPERTS * MOE_ALIGN
MOE_ZERO_ROWS = 256
COPY_SIZES = (512, 256, 128, 64, 32, 16)
FF_CHUNKS = ((0, 1536), (1536, 1280))


def _cparams(*sem):
    return pltpu.CompilerParams(dimension_semantics=sem, vmem_limit_bytes=VMEM_LIMIT)


def _dot(a, b):
    return jnp.dot(a, b, preferred_element_type=F32)


def _dot_nt(a, b):
    return lax.dot_general(a, b, (((1,), (1,)), ((), ())), preferred_element_type=F32)


def _dot_tn(a, b):
    return lax.dot_general(a, b, (((0,), (0,)), ((), ())), preferred_element_type=F32)


def _silu(x):
    return x * (1.0 / (1.0 + jnp.exp(-x)))


def _softplus(x):
    return jnp.maximum(x, 0.0) + jnp.log1p(jnp.exp(-jnp.abs(x)))


def _layer_norm(y, g, b):
    mu = jnp.mean(y, axis=-1, keepdims=True)
    yc = y - mu
    var = jnp.mean(yc * yc, axis=-1, keepdims=True)
    return yc * lax.rsqrt(var + LN_EPS) * g + b


def _dn_inproj_body(x_ref, w_ref, wba_ref, wbat_ref, convw_ref, proj_ref, ba_ref, bat_ref,
                    stage_ref, *, tiles_per_seq):
    hn, dk = DN_HEADS, DN_DK
    rows = x_ref.shape[0]
    width = hn * dk

    @pl.when(pl.program_id(0) % tiles_per_seq == 0)
    def _():
        stage_ref[0:8, :] = jnp.zeros((8, stage_ref.shape[1]), F32)

    xb = x_ref[...].astype(BF16)
    ba_ref[...] = _dot(xb, wba_ref[...])
    bat_ref[...] = _dot_nt(wbat_ref[...], xb)
    for part in range(4):
        raw = _dot(xb, w_ref[:, part * width:(part + 1) * width])
        if part < 3:
            stage_ref[8:8 + rows, part * width:(part + 1) * width] = raw
        for h in range(hn):
            col0 = part * width + h * dk
            if part < 3:
                w = convw_ref[:, col0:col0 + dk]
                a = None
                for j in range(DN_CONV):
                    term = stage_ref[8 - j:8 - j + rows, col0:col0 + dk] * w[DN_CONV - 1 - j:DN_CONV - j]
                    a = term if a is None else a + term
            else:
                a = raw[:, h * dk:(h + 1) * dk]
            a = _silu(a)
            if part < 2:
                scale = lax.rsqrt(jnp.sum(a * a, axis=-1, keepdims=True) + RMS_EPS)
                a = a * (scale * (dk ** -0.5) if part == 0 else scale)
            proj_ref[:, col0:col0 + dk] = a
    stage_ref[0:8, :] = stage_ref[rows:rows + 8, :]


def _dn_inproj(x, w_main, w_ba, w_bat, conv_w, seq):
    n, d = x.shape
    n_out = w_main.shape[1]
    n_qkv = conv_w.shape[1]
    tm = ROW_TILE
    assert seq % tm == 0
    return pl.pallas_call(
        functools.partial(_dn_inproj_body, tiles_per_seq=seq // tm),
        grid=(n // tm,),
        in_specs=[
            pl.BlockSpec((tm, d), lambda i: (i, 0)),
            pl.BlockSpec((d, n_out), lambda i: (0, 0)),
            pl.BlockSpec((d, V7X_LANES), lambda i: (0, 0)),
            pl.BlockSpec((2 * DN_HEADS, d), lambda i: (0, 0)),
            pl.BlockSpec((DN_CONV, n_qkv), lambda i: (0, 0)),
        ],
        out_specs=[
            pl.BlockSpec((tm, n_out), lambda i: (i, 0)),
            pl.BlockSpec((tm, V7X_LANES), lambda i: (i, 0)),
            pl.BlockSpec((2 * DN_HEADS, tm), lambda i: (0, i)),
        ],
        out_shape=[
            jax.ShapeDtypeStruct((n, n_out), F32),
            jax.ShapeDtypeStruct((n, V7X_LANES), F32),
            jax.ShapeDtypeStruct((2 * DN_HEADS, n), F32),
        ],
        scratch_shapes=[pltpu.VMEM((8 + tm, n_qkv), F32)],
        compiler_params=_cparams("arbitrary"),
        name="dn_inproj",
    )(x, w_main, w_ba, w_bat, conv_w)


def _dn_body(proj_ref, ba_ref, bat_ref, gprow_ref, gpcol_ref, normw_ref, o_ref, s_ref):
    rows = proj_ref.shape[0]
    hn, dk, ck = DN_HEADS, DN_DK, DN_CHUNK
    n_chunks = rows // ck
    n_qkv = 3 * hn * dk

    @pl.when(pl.program_id(1) == 0)
    def _():
        s_ref[...] = jnp.zeros_like(s_ref)

    ba = ba_ref[...]
    beta_all = 1.0 / (1.0 + jnp.exp(-ba))
    gprow = gprow_ref[...]
    g_all = -jnp.exp(gprow[0:1]) * _softplus(ba + gprow[1:2])
    row_in_chunk = lax.broadcasted_iota(I32, g_all.shape, 0) & (ck - 1)
    gcum = g_all
    step = 1
    while step < ck:
        gcum = gcum + jnp.where(row_in_chunk >= step, pltpu.roll(gcum, step, 0), 0.0)
        step *= 2
    gpcol = gpcol_ref[...]
    gt = -jnp.exp(gpcol[:, 0:1]) * _softplus(bat_ref[hn:2 * hn, :] + gpcol[:, 1:2])
    lane_in_chunk = lax.broadcasted_iota(I32, gt.shape, 1) & (ck - 1)
    gtcum = gt
    step = 1
    while step < ck:
        gtcum = gtcum + jnp.where(lane_in_chunk >= step, pltpu.roll(gtcum, step, 1), 0.0)
        step *= 2

    ri = lax.broadcasted_iota(I32, (n_chunks, ck, ck), 1)
    ci = lax.broadcasted_iota(I32, (n_chunks, ck, ck), 2)
    strict = ri > ci
    eye = ri == ci

    def chunked(a):
        return a.reshape(n_chunks, ck, a.shape[-1])

    def bmm(a, b):
        return jnp.einsum('cij,cjk->cik', a, b, preferred_element_type=F32)

    def bmm_nt(a, b):
        return jnp.einsum('cid,cjd->cij', a, b, preferred_element_type=F32)

    heads = range(hn)
    gc = [gcum[:, hn + h:hn + h + 1] for h in heads]
    gr = [jnp.stack([gtcum[h:h + 1, c * ck:(c + 1) * ck] for c in range(n_chunks)])
          for h in heads]
    dec = [jnp.exp(jnp.where(strict, chunked(gc[h]) - gr[h], -jnp.inf)) for h in heads]
    eg = [jnp.exp(gc[h]) for h in heads]
    q = [proj_ref[:, h * dk:(h + 1) * dk] for h in heads]
    k = [proj_ref[:, (hn + h) * dk:(hn + h + 1) * dk] for h in heads]
    kb = [k[h] * beta_all[:, h:h + 1] for h in heads]
    vb = [proj_ref[:, (2 * hn + h) * dk:(2 * hn + h + 1) * dk] * beta_all[:, h:h + 1]
          for h in heads]

    kc = [chunked(k[h]).astype(BF16) for h in heads]
    gram = [bmm_nt(jnp.concatenate([chunked(kb[h]), chunked(q[h])], axis=1).astype(BF16), kc[h])
            for h in heads]
    aqk = [(gram[h][:, ck:] * jnp.where(eye, 1.0, dec[h])).astype(BF16) for h in heads]
    left = lax.broadcasted_iota(I32, (n_chunks, ck, 2 * ck), 2) < ck
    x = [jnp.concatenate([-(gram[h][:, 0:ck] * dec[h]), jnp.where(eye, 1.0, 0.0)], axis=2)
         for h in heads]
    for _ in range(6):
        xb = [x[h].astype(BF16) for h in heads]
        bx = [bmm(xb[h][:, :, 0:ck], xb[h]) for h in heads]
        x = [jnp.where(left, bx[h], x[h] + bx[h]) for h in heads]
    uw = [bmm(x[h][:, :, ck:].astype(BF16),
              chunked(jnp.concatenate([vb[h], kb[h] * eg[h]], axis=1)).astype(BF16))
          for h in heads]
    wq = [jnp.concatenate([uw[h][:, :, DN_DV:], chunked(q[h] * eg[h])], axis=2).astype(BF16)
          for h in heads]
    kdec = [[(k[h][c * ck:(c + 1) * ck]
              * jnp.exp(gc[h][(c + 1) * ck - 1:(c + 1) * ck] - gc[h][c * ck:(c + 1) * ck])
              ).astype(BF16) for c in range(n_chunks)] for h in heads]

    s = [s_ref[h] for h in heads]
    zero_s = jnp.zeros((dk, DN_DV), BF16)
    vnew = [[] for _ in heads]
    qs = [[] for _ in heads]
    for c in range(n_chunks):
        r0 = c * ck
        sb = [s[h].astype(BF16) for h in heads]
        s2 = [jnp.concatenate([jnp.concatenate([sb[h], zero_s], axis=1),
                               jnp.concatenate([zero_s, sb[h]], axis=1)], axis=0) for h in heads]
        ws_qs = [_dot(wq[h][c], s2[h]) for h in heads]
        for h in heads:
            vn = uw[h][c][:, 0:DN_DV] - ws_qs[h][:, 0:DN_DV]
            vnew[h].append(vn.astype(BF16))
            qs[h].append(ws_qs[h][:, DN_DV:])
        upd = [_dot_tn(kdec[h][c], vnew[h][c]) for h in heads]
        s = [s[h] * jnp.exp(gc[h][r0 + ck - 1:r0 + ck]) + upd[h] for h in heads]
    for h in heads:
        s_ref[h] = s[h]
    o_all = [(jnp.stack(qs[h]) + bmm(aqk[h], jnp.stack(vnew[h]))).reshape(rows, DN_DV)
             for h in heads]
    for h in heads:
        o = o_all[h]
        o = o * lax.rsqrt(jnp.mean(o * o, axis=-1, keepdims=True) + RMS_EPS) * normw_ref[...]
        zs = proj_ref[:, n_qkv + h * DN_DV:n_qkv + (h + 1) * DN_DV]
        o_ref[:, h * DN_DV:(h + 1) * DN_DV] = (o * zs).astype(o_ref.dtype)


def _dn_cast_body(*refs, n_cast):
    n_in = 6
    ins, casts_in = refs[:n_in], refs[n_in:n_in + n_cast]
    o_ref = refs[n_in + n_cast]
    casts_out = refs[n_in + n_cast + 1:n_in + 2 * n_cast + 1]
    s_ref = refs[-1]
    _dn_body(*ins, o_ref, s_ref)
    for src, dst in zip(casts_in, casts_out):
        dst[...] = src[...].astype(dst.dtype)


def _cast_block_rows(n_rows, n_steps):
    rb = -(-n_rows // n_steps)
    rb = -(-rb // V7X_BF16_SUBLANES) * V7X_BF16_SUBLANES
    while n_rows % rb:
        rb += V7X_BF16_SUBLANES
    return rb


def _deltanet_core(proj, ba, bat, gp_row, gp_col, norm_w, batch, seq, weights_f32):
    n = proj.shape[0]
    r = DN_ROWS
    nt = seq // r
    n_steps = batch * nt
    cast_in, cast_out, cast_shapes = [], [], []
    for w in weights_f32:
        rb = _cast_block_rows(w.shape[0], n_steps)
        last = w.shape[0] // rb - 1
        spec = pl.BlockSpec((rb, w.shape[1]),
                            lambda b, t, last=last: (jnp.minimum(b * nt + t, last), 0))
        cast_in.append(spec)
        cast_out.append(spec)
        cast_shapes.append(jax.ShapeDtypeStruct(w.shape, BF16))
    outs = pl.pallas_call(
        functools.partial(_dn_cast_body, n_cast=len(weights_f32)),
        grid=(batch, nt),
        in_specs=[
            pl.BlockSpec((r, proj.shape[1]), lambda b, t: (b * nt + t, 0)),
            pl.BlockSpec((r, V7X_LANES), lambda b, t: (b * nt + t, 0)),
            pl.BlockSpec((2 * DN_HEADS, r), lambda b, t: (0, b * nt + t)),
            pl.BlockSpec((2, V7X_LANES), lambda b, t: (0, 0)),
            pl.BlockSpec((DN_HEADS, 2), lambda b, t: (0, 0)),
            pl.BlockSpec((1, DN_DV), lambda b, t: (0, 0)),
        ] + cast_in,
        out_specs=[pl.BlockSpec((r, DN_HEADS * DN_DV), lambda b, t: (b * nt + t, 0))] + cast_out,
        out_shape=[jax.ShapeDtypeStruct((n, DN_HEADS * DN_DV), BF16)] + cast_shapes,
        scratch_shapes=[pltpu.VMEM((DN_HEADS, DN_DK, DN_DV), F32)],
        compiler_params=_cparams("arbitrary", "arbitrary"),
        name="deltanet",
    )(proj, ba, bat, gp_row, gp_col, norm_w, *weights_f32)
    return outs[0], outs[1:]


def _swa_body(sink_ref, q_ref, kc_ref, kp_ref, vc_ref, vp_ref, o_ref, bias_ref):
    w, hd = SW_BLOCK, SW_HEAD_DIM
    group = SW_Q_HEADS // SW_KV_HEADS
    heads = range(SW_Q_HEADS)

    @pl.when((pl.program_id(0) == 0) & (pl.program_id(1) == 0))
    def _():
        qi = lax.broadcasted_iota(I32, (w, 2 * w), 0)
        sj = lax.broadcasted_iota(I32, (w, 2 * w), 1)
        dist = qi + w - sj
        valid = (dist >= 0) & (dist < w)
        distf = dist.astype(F32)
        for hq in heads:
            slope = 2.0 ** (-8.0 * (hq + 1) / SW_Q_HEADS)
            bias = jnp.where(valid, -(slope * distf), -jnp.inf)
            bias_ref[1, hq] = bias
            bias_ref[0, hq] = jnp.where(sj >= w, bias, -jnp.inf)

    has_prev = jnp.where(pl.program_id(1) > 0, 1, 0)
    qb = (q_ref[...] * (hd ** -0.5)).astype(BF16)
    kband, vband = [], []
    for kh in range(SW_KV_HEADS):
        cs = slice(kh * hd, (kh + 1) * hd)
        kband.append(jnp.concatenate([kp_ref[:, cs], kc_ref[:, cs]], axis=0).astype(BF16))
        vband.append(jnp.concatenate([vp_ref[:, cs], vc_ref[:, cs]], axis=0).astype(BF16))
    s = [_dot_nt(qb[:, hq * hd:(hq + 1) * hd], kband[hq // group]) + bias_ref[has_prev, hq]
         for hq in heads]
    m = [jnp.maximum(jnp.max(s[hq], axis=-1, keepdims=True), sink_ref[hq]) for hq in heads]
    p = [jnp.exp(s[hq] - m[hq]) for hq in heads]
    den = [jnp.sum(p[hq], axis=-1, keepdims=True) + jnp.exp(sink_ref[hq] - m[hq]) for hq in heads]
    o = [_dot(p[hq].astype(BF16), vband[hq // group]) / den[hq] for hq in heads]
    for hq in heads:
        o_ref[:, hq * hd:(hq + 1) * hd] = o[hq].astype(o_ref.dtype)


def _swa(qkv, sinks, batch, seq):
    n = qkv.shape[0]
    w = SW_BLOCK
    nb = seq // w
    nq = SW_Q_HEADS * SW_HEAD_DIM
    nkv = SW_KV_HEADS * SW_HEAD_DIM
    kcol = nq // nkv
    cur = lambda b, i: b * nb + i
    prev = lambda b, i: b * nb + jnp.maximum(i - 1, 0)
    return pl.pallas_call(
        _swa_body,
        grid=(batch, nb),
        in_specs=[
            pl.BlockSpec(memory_space=pltpu.SMEM),
            pl.BlockSpec((w, nq), lambda b, i: (cur(b, i), 0)),
            pl.BlockSpec((w, nkv), lambda b, i: (cur(b, i), kcol)),
            pl.BlockSpec((w, nkv), lambda b, i: (prev(b, i), kcol)),
            pl.BlockSpec((w, nkv), lambda b, i: (cur(b, i), kcol + 1)),
            pl.BlockSpec((w, nkv), lambda b, i: (prev(b, i), kcol + 1)),
        ],
        out_specs=pl.BlockSpec((w, nq), lambda b, i: (cur(b, i), 0)),
        out_shape=jax.ShapeDtypeStruct((n, nq), BF16),
        scratch_shapes=[pltpu.VMEM((2, SW_Q_HEADS, w, 2 * w), F32)],
        compiler_params=_cparams("arbitrary", "arbitrary"),
        name="swa",
    )(sinks, qkv, qkv, qkv, qkv, qkv)


def _ffn_tile(x, wgu_ref, wd_ref):
    xb = x.astype(BF16)
    d_ff = wd_ref.shape[1]
    y = None
    for c0, cw in FF_CHUNKS:
        hg = _dot(xb, wgu_ref[0, :, c0:c0 + cw])
        hu = _dot(xb, wgu_ref[0, :, d_ff + c0:d_ff + c0 + cw])
        act = (_silu(hg) * hu).astype(BF16)
        part = _dot(act, wd_ref[0, c0:c0 + cw, :])
        y = part if y is None else y + part
    return y


def _ffn_group_body(eid_ref, nt_ref, x_ref, wgu_ref, wd_ref, o_ref):
    used = pl.program_id(0) < nt_ref[0]

    @pl.when(used)
    def _():
        o_ref[...] = _ffn_tile(x_ref[...], wgu_ref, wd_ref).astype(o_ref.dtype)

    @pl.when(jnp.logical_not(used))
    def _():
        o_ref[...] = jnp.zeros_like(o_ref)


def _ffn_grouped(xs, w_gu, w_down, eid, nt):
    r, d = xs.shape
    tm = ROW_TILE
    d_ff = w_down.shape[1]
    assert sum(cw for _, cw in FF_CHUNKS) == d_ff
    tile = lambda i, eid_ref, nt_ref: jnp.minimum(i, nt_ref[0] - 1)
    return pl.pallas_call(
        _ffn_group_body,
        grid_spec=pltpu.PrefetchScalarGridSpec(
            num_scalar_prefetch=2,
            grid=(r // tm,),
            in_specs=[
                pl.BlockSpec((tm, d), lambda i, e, n: (tile(i, e, n), 0)),
                pl.BlockSpec((1, d, 2 * d_ff), lambda i, e, n: (e[tile(i, e, n)], 0, 0)),
                pl.BlockSpec((1, d_ff, d), lambda i, e, n: (e[tile(i, e, n)], 0, 0)),
            ],
            out_specs=pl.BlockSpec((tm, d), lambda i, e, n: (i, 0)),
        ),
        out_shape=jax.ShapeDtypeStruct((r, d), BF16),
        compiler_params=_cparams("arbitrary"),
        name="ffn_grouped",
    )(eid, nt, xs, w_gu, w_down)


def _layer0_tail_body(a_ref, res_ref, wo_ref, ln_ref, wgu_ref, wd_ref, wqkv_ref, x_ref, qkv_ref):
    ln = ln_ref[...]
    x1 = _layer_norm(DEEP_ALPHA * res_ref[...] + _dot(a_ref[...], wo_ref[...]), ln[0:1], ln[1:2])
    x2 = _layer_norm(DEEP_ALPHA * x1 + _ffn_tile(x1, wgu_ref, wd_ref), ln[2:3], ln[3:4])
    x_ref[...] = x2
    qkv_ref[...] = _dot(x2.astype(BF16), wqkv_ref[...])


def _layer0_tail(a, res, w_out, ln, w_gu, w_down, w_qkv):
    n, d = res.shape
    k = a.shape[1]
    m = w_qkv.shape[1]
    tm = ROW_TILE
    d_ff = w_down.shape[1]
    assert sum(cw for _, cw in FF_CHUNKS) == d_ff
    const = lambda shape: pl.BlockSpec(shape, lambda i: (0,) * len(shape),
                                       pipeline_mode=pl.Buffered(1))
    return pl.pallas_call(
        _layer0_tail_body,
        grid=(n // tm,),
        in_specs=[pl.BlockSpec((tm, k), lambda i: (i, 0)),
                  pl.BlockSpec((tm, d), lambda i: (i, 0)),
                  const((k, d)),
                  const((4, d)),
                  const((1, d, 2 * d_ff)),
                  const((1, d_ff, d)),
                  const((d, m))],
        out_specs=[pl.BlockSpec((tm, d), lambda i: (i, 0)),
                   pl.BlockSpec((tm, m), lambda i: (i, 0))],
        out_shape=[jax.ShapeDtypeStruct((n, d), F32),
                   jax.ShapeDtypeStruct((n, m), F32)],
        compiler_params=_cparams("parallel"),
        name="layer0_tail",
    )(a, res, w_out, ln, w_gu, w_down, w_qkv)


def _split3(a):
    hi = a.astype(BF16)
    r1 = a - hi.astype(F32)
    mid = r1.astype(BF16)
    lo = (r1 - mid.astype(F32)).astype(BF16)
    return hi, mid, lo


def _route(x, w_router_t, dest_t_ref, dest_ref, gate_ref, cnt_ref):
    tm = x.shape[0]
    ne = N_EXPERTS
    xh, xm, xl = _split3(x)
    wst = jnp.concatenate(_split3(w_router_t), axis=0)
    th = _dot_nt(wst, xh)
    tmid = _dot_nt(wst[0:2 * ne], xm)
    tl = _dot_nt(wst[0:ne], xl)
    logits = ((th[2 * ne:] + tmid[ne:] + tl) + (th[ne:2 * ne] + tmid[0:ne])) + th[0:ne]

    eidx = lax.broadcasted_iota(I32, (ne, tm), 0)
    m1 = jnp.max(logits, axis=0, keepdims=True)
    i1 = jnp.min(jnp.where(logits == m1, eidx, ne), axis=0, keepdims=True)
    mask1 = eidx == i1
    rest = jnp.where(mask1, -jnp.inf, logits)
    m2 = jnp.max(rest, axis=0, keepdims=True)
    i2 = jnp.min(jnp.where(rest == m2, eidx, ne), axis=0, keepdims=True)
    mask2 = eidx == i2
    ex = jnp.exp(m2 - m1)
    w1 = 1.0 / (1.0 + ex)
    w2 = ex / (1.0 + ex)
    sel = mask1 | mask2

    ti = lax.broadcasted_iota(I32, (tm, tm), 0)
    tj = lax.broadcasted_iota(I32, (tm, tm), 1)
    before = jnp.where(ti < tj, 1.0, 0.0).astype(BF16)
    self_f = jnp.where(sel, 1.0, 0.0)
    rank = _dot(self_f.astype(BF16), before)
    cnt = jnp.sum(self_f, axis=1, keepdims=True).astype(I32)
    cnt_ref[0] = jnp.broadcast_to(cnt, (ne, V7X_LANES))

    seg_rows = (((cnt + (MOE_ALIGN - 1)) // MOE_ALIGN) * MOE_ALIGN).astype(F32)
    seg_rows = jnp.broadcast_to(seg_rows, (ne, V7X_LANES))
    e8 = lax.broadcasted_iota(I32, (ne, V7X_LANES), 0)
    seg0 = jnp.zeros((ne, V7X_LANES), F32)
    for k in range(1, ne):
        seg0 = seg0 + jnp.where(e8 >= k, pltpu.roll(seg_rows, k, 0), 0.0)
    dest = seg0[:, 0:1] + rank
    d1 = jnp.sum(jnp.where(mask1, dest, 0.0), axis=0, keepdims=True)
    d2 = jnp.sum(jnp.where(mask2, dest, 0.0), axis=0, keepdims=True)
    dest_t_ref[...] = jnp.concatenate([d1, d2], axis=0).astype(I32)

    d1i, d2i = d1.astype(I32), d2.astype(I32)
    pieces = [(d1i >> 8).astype(F32), (d2i >> 8).astype(F32),
              (d1i & 255).astype(F32), (d2i & 255).astype(F32)]
    pieces = [p.astype(BF16) for p in pieces]
    for a, b in zip(_split3(w1), _split3(w2)):
        pieces += [a, b]
    pieces.append(jnp.zeros((16 - len(pieces), tm), BF16))
    ident = jnp.where(ti == tj, 1.0, 0.0).astype(BF16)
    cols = _dot_nt(ident, jnp.concatenate(pieces, axis=0))
    dest_ref[...] = (cols[:, 0:2] * 256.0 + cols[:, 2:4]).astype(I32)
    gate_ref[...] = cols[:, 4:6] + (cols[:, 6:8] + cols[:, 8:10])


def _attn_out_route_body(a_ref, w_ref, res_ref, g_ref, b_ref, wr_ref,
                         x_ref, dest_t_ref, dest_ref, gate_ref, cnt_ref):
    x = _layer_norm(DEEP_ALPHA * res_ref[...] + _dot(a_ref[...], w_ref[...]), g_ref[...], b_ref[...])
    x_ref[...] = x
    _route(x, wr_ref[...], dest_t_ref, dest_ref, gate_ref, cnt_ref)


def _attn_out_route(a, w, res, g, b, w_router_t):
    n, k = a.shape
    d = w.shape[1]
    tm = MOE_TILE
    ne = N_EXPERTS
    nt = n // tm
    return pl.pallas_call(
        _attn_out_route_body,
        grid=(nt,),
        in_specs=[pl.BlockSpec((tm, k), lambda i: (i, 0)),
                  pl.BlockSpec((k, d), lambda i: (0, 0)),
                  pl.BlockSpec((tm, d), lambda i: (i, 0)),
                  pl.BlockSpec((1, d), lambda i: (0, 0)),
                  pl.BlockSpec((1, d), lambda i: (0, 0)),
                  pl.BlockSpec((ne, d), lambda i: (0, 0))],
        out_specs=[pl.BlockSpec((tm, d), lambda i: (i, 0)),
                   pl.BlockSpec((TOP_K, tm), lambda i: (0, i)),
                   pl.BlockSpec((tm, TOP_K), lambda i: (i, 0)),
                   pl.BlockSpec((tm, TOP_K), lambda i: (i, 0)),
                   pl.BlockSpec((1, ne, V7X_LANES), lambda i: (i, 0, 0))],
        out_shape=[jax.ShapeDtypeStruct((n, d), F32),
                   jax.ShapeDtypeStruct((TOP_K, n), I32),
                   jax.ShapeDtypeStruct((n, TOP_K), I32),
                   jax.ShapeDtypeStruct((n, TOP_K), F32),
                   jax.ShapeDtypeStruct((nt, ne, V7X_LANES), I32)],
        compiler_params=_cparams("parallel"),
        name="attn_out_route",
    )(a, w, res, g, b, w_router_t)


def _segment_copies(tile, pc_ref, seg0_ref, off_ref, make_copy):
    out = []
    for e in range(N_EXPERTS):
        rows = pc_ref[tile * N_EXPERTS + e]
        packed0 = seg0_ref[tile * N_EXPERTS + e]
        sorted0 = off_ref[tile * N_EXPERTS + e]
        for k, size in enumerate(COPY_SIZES):
            offset = rows & ~(2 * size - 1)
            out.append(((rows & size) != 0,
                        make_copy(pl.multiple_of(packed0 + offset, MOE_ALIGN),
                                  pl.multiple_of(sorted0 + offset, MOE_ALIGN), size, e, k)))
    return out


def _start_all(copies):
    for cond, cp in copies:
        pl.when(cond)(cp.start)


def _wait_all(copies):
    for cond, cp in copies:
        pl.when(cond)(cp.wait)


def _zero_rows(zbuf, dst_hbm, row0, n_rows, sem):
    zrows = zbuf.shape[0]

    def whole(kc, carry):
        cp = pltpu.make_async_copy(
            zbuf, dst_hbm.at[pl.ds(pl.multiple_of(row0 + kc * zrows, MOE_ALIGN), zrows)], sem.at[0])
        cp.start()
        cp.wait()
        return carry
    n_whole = n_rows // zrows
    lax.fori_loop(0, n_whole, whole, 0)
    rest = n_rows - n_whole * zrows
    base = row0 + n_whole * zrows
    copies = []
    for k, size in enumerate(s for s in COPY_SIZES if s < zrows):
        offset = rest & ~(2 * size - 1)
        copies.append(((rest & size) != 0, pltpu.make_async_copy(
            zbuf.at[pl.ds(0, size)],
            dst_hbm.at[pl.ds(pl.multiple_of(base + offset, MOE_ALIGN), size)], sem.at[k])))
    _start_all(copies)
    _wait_all(copies)


def _dispatch_body(off_ref, pc_ref, seg0_ref, zoff_ref, zlen_ref, x_ref, dest_t_ref, xs_hbm,
                   buf, zbuf, sem, zsem, *, n_steps):
    t = pl.program_id(0)
    slot = t % 2
    tm = x_ref.shape[0]

    def copies(tile, sl):
        def mk(packed0, sorted0, size, e, k):
            return pltpu.make_async_copy(buf.at[sl, pl.ds(packed0, size)],
                                         xs_hbm.at[pl.ds(sorted0, size)], sem.at[sl, e, k])
        return _segment_copies(tile, pc_ref, seg0_ref, off_ref, mk)

    @pl.when(t >= 2)
    def _():
        _wait_all(copies(t - 2, slot))

    dest = dest_t_ref[...]
    j = lax.broadcasted_iota(I32, (MOE_PACK, tm), 0)
    sel = jnp.where((j == dest[0:1]) | (j == dest[1:2]), 1.0, 0.0).astype(BF16)
    buf[slot] = _dot(sel, x_ref[...].astype(BF16)).astype(BF16)
    _start_all(copies(t, slot))

    @pl.when(t == n_steps - 1)
    def _():
        if n_steps > 1:
            _wait_all(copies(t - 1, 1 - slot))
        _wait_all(copies(t, slot))
        zbuf[...] = jnp.zeros_like(zbuf)
        for e in range(N_EXPERTS):
            _zero_rows(zbuf, xs_hbm, zoff_ref[e], zlen_ref[e], zsem)
        used_rows = zoff_ref[N_EXPERTS - 1] + zlen_ref[N_EXPERTS - 1]
        _zero_rows(zbuf, xs_hbm, used_rows, xs_hbm.shape[0] - used_rows, zsem)


def _dispatch(x, dest_t, off, pc, seg0, zoff, zlen, n_rows):
    n, d = x.shape
    tm = MOE_TILE
    n_sizes = len(COPY_SIZES)
    return pl.pallas_call(
        functools.partial(_dispatch_body, n_steps=n // tm),
        grid_spec=pltpu.PrefetchScalarGridSpec(
            num_scalar_prefetch=5,
            grid=(n // tm,),
            in_specs=[pl.BlockSpec((tm, d), lambda i, *_: (i, 0)),
                      pl.BlockSpec((TOP_K, tm), lambda i, *_: (0, i))],
            out_specs=pl.BlockSpec(memory_space=pl.ANY),
            scratch_shapes=[pltpu.VMEM((2, MOE_PACK, d), BF16),
                            pltpu.VMEM((MOE_ZERO_ROWS, d), BF16),
                            pltpu.SemaphoreType.DMA((2, N_EXPERTS, n_sizes)),
                            pltpu.SemaphoreType.DMA((n_sizes,))],
        ),
        out_shape=jax.ShapeDtypeStruct((n_rows, d), BF16),
        compiler_params=_cparams("arbitrary"),
        name="moe_dispatch",
    )(off, pc, seg0, zoff, zlen, x, dest_t)


def _combine_body(off_ref, pc_ref, seg0_ref, x_ref, dest_ref, gate_ref, g_ref, b_ref, ys_hbm,
                  o_ref, ybuf, sem, *, n_steps):
    t = pl.program_id(0)
    slot = t % 2
    tm = x_ref.shape[0]

    def copies(tile, sl):
        def mk(packed0, sorted0, size, e, k):
            return pltpu.make_async_copy(ys_hbm.at[pl.ds(sorted0, size)],
                                         ybuf.at[sl, pl.ds(packed0, size)], sem.at[sl, e, k])
        return _segment_copies(tile, pc_ref, seg0_ref, off_ref, mk)

    @pl.when(t == 0)
    def _():
        ybuf[...] = jnp.zeros_like(ybuf)
        _start_all(copies(0, 0))

    @pl.when(t + 1 < n_steps)
    def _():
        _start_all(copies(t + 1, 1 - slot))

    _wait_all(copies(t, slot))
    dest = dest_ref[...]
    gate = gate_ref[...]
    y = ybuf[slot]
    lane = lax.broadcasted_iota(I32, (tm, MOE_PACK), 1)
    mix = None
    for c in range(TOP_K):
        pick = jnp.where(lane == dest[:, c:c + 1], 1.0, 0.0).astype(BF16)
        term = gate[:, c:c + 1] * _dot(pick, y)
        mix = term if mix is None else mix + term
    o_ref[...] = _layer_norm(DEEP_ALPHA * x_ref[...] + mix, g_ref[...], b_ref[...])


def _combine_ln(x, dest, gate, g, b, ys, off, pc, seg0):
    n, d = x.shape
    tm = MOE_TILE
    return pl.pallas_call(
        functools.partial(_combine_body, n_steps=n // tm),
        grid_spec=pltpu.PrefetchScalarGridSpec(
            num_scalar_prefetch=3,
            grid=(n // tm,),
            in_specs=[pl.BlockSpec((tm, d), lambda i, *_: (i, 0)),
                      pl.BlockSpec((tm, TOP_K), lambda i, *_: (i, 0)),
                      pl.BlockSpec((tm, TOP_K), lambda i, *_: (i, 0)),
                      pl.BlockSpec((1, d), lambda i, *_: (0, 0)),
                      pl.BlockSpec((1, d), lambda i, *_: (0, 0)),
                      pl.BlockSpec(memory_space=pl.ANY)],
            out_specs=pl.BlockSpec((tm, d), lambda i, *_: (i, 0)),
            scratch_shapes=[pltpu.VMEM((2, MOE_PACK, d), BF16),
                            pltpu.SemaphoreType.DMA((2, N_EXPERTS, len(COPY_SIZES)))],
        ),
        out_shape=jax.ShapeDtypeStruct((n, d), F32),
        compiler_params=_cparams("arbitrary"),
        name="moe_combine_ln",
    )(off, pc, seg0, x, dest, gate, g, b, ys)


def _moe_plan(cnt, n_tiles_max):
    tmf = ROW_TILE
    pc = (cnt + (MOE_ALIGN - 1)) // MOE_ALIGN * MOE_ALIGN
    seg0 = jnp.cumsum(pc, axis=1) - pc
    seg = jnp.sum(pc, axis=0)
    reg = (seg + (tmf - 1)) // tmf * tmf
    base = jnp.cumsum(reg) - reg
    off = base[None, :] + jnp.cumsum(pc, axis=0) - pc
    tile_end = jnp.cumsum(reg // tmf)
    nt = tile_end[-1:]
    tiles = jnp.arange(n_tiles_max, dtype=I32)
    eid = jnp.minimum(jnp.sum((tile_end[None, :] <= tiles[:, None]).astype(I32), axis=1),
                      N_EXPERTS - 1)
    flat = lambda a: a.reshape(-1).astype(I32)
    return (flat(off), flat(pc), flat(seg0), flat(base + seg), flat(reg - seg), flat(eid), flat(nt))


def _moe_ln(x, routing, w_gu, w_down, g, b):
    n, d = x.shape
    n_tok_tiles = n // MOE_TILE
    dest_t, dest, gate, cnt = routing
    max_rows = 2 * n + (MOE_ALIGN - 1) * N_EXPERTS * n_tok_tiles + N_EXPERTS * (ROW_TILE - MOE_ALIGN)
    n_tiles_max = -(-max_rows // ROW_TILE)
    off, pc, seg0, zoff, zlen, eid, nt = _moe_plan(cnt[:, :, 0], n_tiles_max)
    xs = _dispatch(x, dest_t, off, pc, seg0, zoff, zlen, n_tiles_max * ROW_TILE)
    ys = _ffn_grouped(xs, w_gu, w_down, eid, nt)
    return _combine_ln(x, dest, gate, g, b, ys, off, pc, seg0)


def kernel(x, dn_w_in, dn_conv_w, dn_a_log, dn_dt_bias, dn_norm_w, dn_w_out, w_kv_shared,
           sw_w_q, sw_sinks, sw_w_out, ffn_w_gu, ffn_w_down, moe_w_router, moe_w_gu,
           moe_w_down, ln_g, ln_b):
    batch, seq, d = x.shape
    n = batch * seq
    hn = DN_HEADS
    assert seq % DN_ROWS == 0 and n % ROW_TILE == 0 and n % MOE_TILE == 0
    assert dn_w_in.shape[0] == 1 and sw_w_q.shape[0] == 1
    xf = x.reshape(n, d)
    row = lambda v: v.reshape(1, -1).astype(F32)

    n_main = 4 * hn * DN_DK
    w_in = dn_w_in[0]
    w_main = w_in[:, :n_main].astype(BF16)
    w_ba = jnp.pad(w_in[:, n_main:], ((0, 0), (0, V7X_LANES - 2 * hn))).astype(BF16)
    w_bat = w_in[:, n_main:].T.astype(BF16)
    proj, ba, bat = _dn_inproj(xf, w_main, w_ba, w_bat, dn_conv_w[0].astype(F32), seq)
    gate_params = jnp.stack([dn_a_log[0], dn_dt_bias[0]]).astype(F32)
    gp_row = jnp.pad(gate_params, ((0, 0), (hn, V7X_LANES - 2 * hn)))
    later = [ffn_w_gu[0], ffn_w_down[0], moe_w_gu[0], moe_w_down[0]]
    o, later_bf16 = _deltanet_core(proj, ba, bat, gp_row, gate_params.T, row(dn_norm_w[0]),
                                   batch, seq, [w.reshape(-1, w.shape[-1]) for w in later])
    ffn_gu, ffn_down, moe_gu, moe_down = [
        wb.reshape((1,) + w.shape if w.ndim == 2 else w.shape) for wb, w in zip(later_bf16, later)]
    w_qkv = jnp.concatenate([sw_w_q[0], w_kv_shared], axis=1).astype(BF16)
    ln0 = jnp.concatenate([ln_g[0, 0:1], ln_b[0, 0:1], ln_g[0, 1:2], ln_b[0, 1:2]]).astype(F32)
    x2, qkv = _layer0_tail(o, xf, dn_w_out[0].astype(BF16), ln0, ffn_gu, ffn_down, w_qkv)

    att = _swa(qkv, sw_sinks[0].astype(F32), batch, seq)
    x3, *routing = _attn_out_route(att, sw_w_out[0].astype(BF16), x2, row(ln_g[1, 0]),
                                   row(ln_b[1, 0]), moe_w_router[0].T.astype(F32))
    x4 = _moe_ln(x3, routing, moe_gu, moe_down, row(ln_g[1, 1]), row(ln_b[1, 1]))
    return x4.reshape(batch, seq, d)
```
